```python
import math
import jax
import jax.numpy as jnp
from jax import lax
import numpy as np

D_MODEL = 1024
BATCH = 8
SEQ = 2048
DEPTH = 4
DEC_BATCH = 32
DEC_SEQ = 8
PAST_LEN = 8192
PAGE_SIZE = 128

N_A_LAYERS = DEPTH // 2
N_B_LAYERS = DEPTH - N_A_LAYERS
SSM_GROUP = 16
SSM_GROUPS = D_MODEL // SSM_GROUP
SSM_STATE = 64
N_HEADS = 8
HEAD_DK = 64
HEAD_DV = 2 * HEAD_DK
D_FF = -(-(-(-8 * D_MODEL // 3)) // 256) * 256
Q_BLOCK = 128
EPS = 1e-6
DT_MIN = 1e-3
DT_MAX = 1e-1
ATTN_SCALE = HEAD_DK ** -0.5
F32 = jnp.float32

kernel_name = 'yoco_s5_diffattn_adaln_step'


def rms_norm(x, g):
    xf = x.astype(F32)
    y = xf * lax.rsqrt(jnp.mean(xf * xf, axis=-1, keepdims=True) + EPS)
    return (y * g.astype(F32)).astype(x.dtype)


def ada_mod(c, w, b, n):
    m = jax.nn.silu(c) @ w + b
    return jnp.split(m[:, None, :], n, axis=-1)


def modulate(x, g, shift, scale):
    return rms_norm(x, g) * (1.0 + scale) + shift


def ssm_discretize(lam_re, lam_im, log_step, b_re, b_im):
    dt = jnp.exp(log_step.astype(F32))[:, None]
    lr, li = lam_re.astype(F32), lam_im.astype(F32)
    mag = jnp.exp(lr * dt)
    a_re, a_im = mag * jnp.cos(li * dt), mag * jnp.sin(li * dt)
    nr, ni = a_re - 1.0, a_im
    den = lr * lr + li * li
    f_re = (nr * lr + ni * li) / den
    f_im = (ni * lr - nr * li) / den
    br, bi = b_re.astype(F32), b_im.astype(F32)
    bb_re = f_re[..., None] * br - f_im[..., None] * bi
    bb_im = f_re[..., None] * bi + f_im[..., None] * br
    return a_re, a_im, bb_re, bb_im


def _ssm_combine(e1, e2):
    a1r, a1i, b1r, b1i = e1
    a2r, a2i, b2r, b2i = e2
    return (a2r * a1r - a2i * a1i,
            a2r * a1i + a2i * a1r,
            a2r * b1r - a2i * b1i + b2r,
            a2r * b1i + a2i * b1r + b2i)


def s5_mixer(u, h0_re, h0_im, lam_re, lam_im, log_step, b_re, b_im, c_re, c_im, d_skip, w_glu):
    bsz, L, _ = u.shape
    uf = u.astype(F32)
    ug = uf.reshape(bsz, L, SSM_GROUPS, SSM_GROUP)
    a_re, a_im, bb_re, bb_im = ssm_discretize(lam_re, lam_im, log_step, b_re, b_im)
    bu_re = jnp.einsum('blgp,gnp->blgn', ug, bb_re)
    bu_im = jnp.einsum('blgp,gnp->blgn', ug, bb_im)
    bu_re = bu_re.at[:, 0].add(a_re * h0_re - a_im * h0_im)
    bu_im = bu_im.at[:, 0].add(a_re * h0_im + a_im * h0_re)
    shape = bu_re.shape
    _, _, h_re, h_im = lax.associative_scan(
        _ssm_combine,
        (jnp.broadcast_to(a_re, shape), jnp.broadcast_to(a_im, shape), bu_re, bu_im),
        axis=1)
    y = (jnp.einsum('blgn,gpn->blgp', h_re, c_re.astype(F32))
         - jnp.einsum('blgn,gpn->blgp', h_im, c_im.astype(F32)))
    y = y.reshape(bsz, L, D_MODEL) + d_skip.astype(F32) * uf
    g = jax.nn.gelu(y)
    za, zb = jnp.split(g @ w_glu.astype(F32), 2, axis=-1)
    out = za * jax.nn.sigmoid(zb)
    return out.astype(u.dtype), h_re[:, -1], h_im[:, -1]


def swiglu_ffn(h, w13, w2):
    a, b = jnp.split(h @ w13, 2, axis=-1)
    return (jax.nn.silu(a) * b) @ w2


def shared_kv(x, c, norm_kv, w_ada_kv, b_ada_kv, w_kv, k_norm):
    bsz, L, _ = x.shape
    shift, scale = ada_mod(c, w_ada_kv, b_ada_kv, 2)
    h = modulate(x, norm_kv, shift, scale)
    k, v = jnp.split(h @ w_kv, [N_HEADS * 2 * HEAD_DK], axis=-1)
    k = rms_norm(k.reshape(bsz, L, N_HEADS, 2, HEAD_DK), k_norm)
    return k, v.reshape(bsz, L, N_HEADS, HEAD_DV)


def diff_attn_prompt(q, k, v, lam):
    bsz, L = q.shape[:2]
    nb = L // Q_BLOCK
    qb = q.reshape(bsz, nb, Q_BLOCK, N_HEADS, 2, HEAD_DK).transpose(1, 0, 2, 3, 4, 5)
    kpos = jnp.arange(L)

    def block(args):
        qi, i = args
        s = jnp.einsum('bqhmd,bkhmd->bhmqk', qi, k, preferred_element_type=F32) * ATTN_SCALE
        qpos = i * Q_BLOCK + jnp.arange(Q_BLOCK)
        s = jnp.where(kpos[None, :] <= qpos[:, None], s, -jnp.inf)
        p = jax.nn.softmax(s, axis=-1)
        w = p[:, :, 0] - lam * p[:, :, 1]
        return jnp.einsum('bhqk,bkhd->bqhd', w.astype(v.dtype), v)

    o = lax.map(block, (qb, jnp.arange(nb)))
    return o.transpose(1, 0, 2, 3, 4).reshape(bsz, L, N_HEADS, HEAD_DV)


def diff_attn_sample(q, k_past, v_past, k_new, v_new, lam):
    T = q.shape[1]
    P = k_past.shape[1]
    s_past = jnp.einsum('bqhmd,bkhmd->bhmqk', q, k_past, preferred_element_type=F32) * ATTN_SCALE
    s_new = jnp.einsum('bqhmd,bkhmd->bhmqk', q, k_new, preferred_element_type=F32) * ATTN_SCALE
    causal = jnp.arange(T)[None, :] <= jnp.arange(T)[:, None]
    s_new = jnp.where(causal, s_new, -jnp.inf)
    p = jax.nn.softmax(jnp.concatenate([s_past, s_new], axis=-1), axis=-1)
    w = (p[:, :, 0] - lam * p[:, :, 1]).astype(v_new.dtype)
    return (jnp.einsum('bhqk,bkhd->bqhd', w[..., :P], v_past)
            + jnp.einsum('bhqk,bkhd->bqhd', w[..., P:], v_new))


def diff_attention_layer(h, k, v, past, w_q, q_norm, lam_p, subln, w_o, lam_init):
    bsz, L, _ = h.shape
    q = rms_norm((h @ w_q).reshape(bsz, L, N_HEADS, 2, HEAD_DK), q_norm)
    lp = lam_p.astype(F32)
    lam = jnp.exp(jnp.sum(lp[0] * lp[1])) - jnp.exp(jnp.sum(lp[2] * lp[3])) + lam_init
    if past is None:
        o = diff_attn_prompt(q, k, v, lam)
    else:
        o = diff_attn_sample(q, past[0], past[1], k, v, lam)
    o = rms_norm(o, subln) * (1.0 - lam_init)
    return o.reshape(bsz, L, N_HEADS * HEAD_DV) @ w_o


def run_group(x, c, h0_re, h0_im, past, p):
    new_re, new_im = [], []
    k = v = None
    for layer in range(DEPTH):
        sh, sc, gt = ada_mod(c, p['w_ada_mix'][layer], p['b_ada_mix'][layer], 3)
        h = modulate(x, p['norm_mix'][layer], sh, sc)
        if layer < N_A_LAYERS:
            a = layer
            out, hr, hi = s5_mixer(h, h0_re[a], h0_im[a], p['ssm_lam_re'][a], p['ssm_lam_im'][a],
                                   p['ssm_log_step'][a], p['ssm_b_re'][a], p['ssm_b_im'][a],
                                   p['ssm_c_re'][a], p['ssm_c_im'][a], p['ssm_d'][a], p['ssm_w_glu'][a])
            new_re.append(hr)
            new_im.append(hi)
        else:
            j = layer - N_A_LAYERS
            lam_init = 0.8 - 0.6 * math.exp(-0.3 * layer)
            out = diff_attention_layer(h, k, v, past, p['w_q'][j], p['q_norm'][j], p['diff_lambda'][j],
                                       p['subln'][j], p['w_o'][j], lam_init)
        x = x + gt * out
        sh, sc, gt = ada_mod(c, p['w_ada_ffn'][layer], p['b_ada_ffn'][layer], 3)
        h = modulate(x, p['norm_ffn'][layer], sh, sc)
        x = x + gt * swiglu_ffn(h, p['ffn_w13'][layer], p['ffn_w2'][layer])
        if layer == N_A_LAYERS - 1:
            k, v = shared_kv(x, c, p['norm_kv'], p['w_ada_kv'], p['b_ada_kv'], p['w_kv'], p['k_norm'])
    return x, jnp.stack(new_re), jnp.stack(new_im), k, v


def setup_inputs(seed: int = 0) -> dict:
    key = jax.random.key(seed)
    ks = iter(jax.random.split(key, 48))

    def nrm(shape, s):
        return jax.random.normal(next(ks), shape, F32) * s

    D, G, N, P, H = D_MODEL, SSM_GROUPS, SSM_STATE, SSM_GROUP, N_HEADS
    n_pages = PAST_LEN // PAGE_SIZE
    n_used = DEC_BATCH * n_pages
    n_pool = n_used + max(1, n_used // 4)
    ada_s = 0.5 * D ** -0.5
    inp = {}
    inp['x_prompt'] = nrm((BATCH, SEQ, D), 1.0)
    inp['x_sample'] = nrm((DEC_BATCH, DEC_SEQ, D), 1.0)
    inp['state_ssm_re'] = nrm((N_A_LAYERS, DEC_BATCH, G, N), 0.3)
    inp['state_ssm_im'] = nrm((N_A_LAYERS, DEC_BATCH, G, N), 0.3)
    inp['cache_k'] = nrm((n_pool, PAGE_SIZE, H, 2, HEAD_DK), 1.0)
    inp['cache_v'] = nrm((n_pool, PAGE_SIZE, H, HEAD_DV), 1.0)
    inp['page_table'] = jax.random.permutation(next(ks), n_pool)[:n_used].reshape(
        DEC_BATCH, n_pages).astype(jnp.int32)
    inp['c_prompt'] = nrm((BATCH, D), 1.0)
    inp['c_sample'] = nrm((DEC_BATCH, D), 1.0)
    inp['norm_mix'] = 1.0 + nrm((DEPTH, D), 0.02)
    inp['w_ada_mix'] = nrm((DEPTH, D, 3 * D), ada_s)
    inp['b_ada_mix'] = nrm((DEPTH, 3 * D), 0.02)
    inp['norm_ffn'] = 1.0 + nrm((DEPTH, D), 0.02)
    inp['w_ada_ffn'] = nrm((DEPTH, D, 3 * D), ada_s)
    inp['b_ada_ffn'] = nrm((DEPTH, 3 * D), 0.02)
    inp['ffn_w13'] = nrm((DEPTH, D, 2 * D_FF), D ** -0.5)
    inp['ffn_w2'] = nrm((DEPTH, D_FF, D), D_FF ** -0.5)
    inp['ssm_lam_re'] = -0.5 + nrm((N_A_LAYERS, G, N), 0.01)
    inp['ssm_lam_im'] = jnp.pi * jnp.arange(N, dtype=F32) + nrm((N_A_LAYERS, G, N), 0.01)
    inp['ssm_log_step'] = jax.random.uniform(next(ks), (N_A_LAYERS, G), F32,
                                             math.log(DT_MIN), math.log(DT_MAX))
    inp['ssm_b_re'] = nrm((N_A_LAYERS, G, N, P), (2 * P) ** -0.5)
    inp['ssm_b_im'] = nrm((N_A_LAYERS, G, N, P), (2 * P) ** -0.5)
    inp['ssm_c_re'] = nrm((N_A_LAYERS, G, P, N), (2 * N) ** -0.5)
    inp['ssm_c_im'] = nrm((N_A_LAYERS, G, P, N), (2 * N) ** -0.5)
    inp['ssm_d'] = nrm((N_A_LAYERS, D), 1.0)
    inp['ssm_w_glu'] = nrm((N_A_LAYERS, D, 2 * D), D ** -0.5)
    inp['norm_kv'] = 1.0 + nrm((D,), 0.02)
    inp['w_ada_kv'] = nrm((D, 2 * D), ada_s)
    inp['b_ada_kv'] = nrm((2 * D,), 0.02)
    inp['w_kv'] = nrm((D, H * 2 * HEAD_DK + H * HEAD_DV), D ** -0.5)
    inp['k_norm'] = 1.0 + nrm((2, HEAD_DK), 0.02)
    inp['w_q'] = nrm((N_B_LAYERS, D, H * 2 * HEAD_DK), D ** -0.5)
    inp['q_norm'] = 1.0 + nrm((N_B_LAYERS, 2, HEAD_DK), 0.02)
    inp['diff_lambda'] = nrm((N_B_LAYERS, 4, HEAD_DK), 0.1)
    inp['subln'] = 1.0 + nrm((N_B_LAYERS, HEAD_DV), 0.02)
    inp['w_o'] = nrm((N_B_LAYERS, H * HEAD_DV, D), (H * HEAD_DV) ** -0.5)
    return inp


def reference(x_prompt, x_sample, state_ssm_re, state_ssm_im, cache_k, cache_v, page_table,
              c_prompt, c_sample, norm_mix, w_ada_mix, b_ada_mix, norm_ffn, w_ada_ffn, b_ada_ffn,
              ffn_w13, ffn_w2, ssm_lam_re, ssm_lam_im, ssm_log_step, ssm_b_re, ssm_b_im,
              ssm_c_re, ssm_c_im, ssm_d, ssm_w_glu, norm_kv, w_ada_kv, b_ada_kv, w_kv, k_norm,
              w_q, q_norm, diff_lambda, subln, w_o):
    p = {'norm_mix': norm_mix, 'w_ada_mix': w_ada_mix, 'b_ada_mix': b_ada_mix,
         'norm_ffn': norm_ffn, 'w_ada_ffn': w_ada_ffn, 'b_ada_ffn': b_ada_ffn,
         'ffn_w13': ffn_w13, 'ffn_w2': ffn_w2,
         'ssm_lam_re': ssm_lam_re, 'ssm_lam_im': ssm_lam_im, 'ssm_log_step': ssm_log_step,
         'ssm_b_re': ssm_b_re, 'ssm_b_im': ssm_b_im, 'ssm_c_re': ssm_c_re, 'ssm_c_im': ssm_c_im,
         'ssm_d': ssm_d, 'ssm_w_glu': ssm_w_glu,
         'norm_kv': norm_kv, 'w_ada_kv': w_ada_kv, 'b_ada_kv': b_ada_kv, 'w_kv': w_kv, 'k_norm': k_norm,
         'w_q': w_q, 'q_norm': q_norm, 'diff_lambda': diff_lambda, 'subln': subln, 'w_o': w_o}

    bsz = x_prompt.shape[0]
    h0 = jnp.zeros((N_A_LAYERS, bsz, SSM_GROUPS, SSM_STATE), F32)
    y_prompt, ssm_re_prompt, ssm_im_prompt, k_prompt, v_prompt = run_group(
        x_prompt, c_prompt, h0, h0, None, p)

    db, n_pages = page_table.shape
    past_len = n_pages * cache_k.shape[1]
    k_past = cache_k[page_table].reshape(db, past_len, N_HEADS, 2, HEAD_DK)
    v_past = cache_v[page_table].reshape(db, past_len, N_HEADS, HEAD_DV)
    y_sample, ssm_re_sample, ssm_im_sample, k_sample, v_sample = run_group(
        x_sample, c_sample, state_ssm_re.astype(F32), state_ssm_im.astype(F32), (k_past, v_past), p)

    return (y_prompt, y_sample, ssm_re_prompt, ssm_im_prompt, ssm_re_sample, ssm_im_sample,
            k_prompt, v_prompt, k_sample, v_sample)
```

```python
import functools
import math

import jax
import jax.numpy as jnp
from jax import lax
from jax.experimental import pallas as pl
from jax.experimental.pallas import tpu as pltpu

F32 = jnp.float32
BF16 = jnp.bfloat16

EPS = 1e-6
N_HEADS = 8
HEAD_DK = 64
HEAD_DV = 2 * HEAD_DK
ATTN_SCALE = HEAD_DK ** -0.5
SSM_GROUP = 16
SSM_STATE = 64
LANES = 128
MXU_DIM = 256
GROUPS_PER_BLOCK = MXU_DIM // SSM_GROUP
VMEM_LIMIT = 56 * 1024 * 1024
NEG_BIG = -1e30


def _params(n_grid, vmem=VMEM_LIMIT):
    return pltpu.CompilerParams(dimension_semantics=("arbitrary",) * n_grid,
                                vmem_limit_bytes=vmem)


def _const_spec(shape):
    nd = len(shape)
    return pl.BlockSpec(tuple(shape), lambda *_: (0,) * nd, pipeline_mode=pl.Buffered(1))


def _dot(a, b):
    return jnp.dot(a, b, preferred_element_type=F32)


def _dot_nt(a, b):
    return lax.dot_general(a, b, (((1,), (1,)), ((), ())), preferred_element_type=F32)


def _modulate(x, g, shift, scale):
    ms = jnp.mean(x * x, axis=-1, keepdims=True)
    xn = x * lax.rsqrt(ms + EPS) * g
    return xn * (1.0 + scale) + shift


def _group_rms(y, ones_blk, w):
    parts = []
    for j in range(y.shape[1] // MXU_DIM):
        blk = y[:, j * MXU_DIM:(j + 1) * MXU_DIM]
        parts.append(_dot((blk * blk).astype(BF16), ones_blk))
    ms = jnp.concatenate(parts, axis=1)
    return y * lax.rsqrt(ms + EPS) * w


def _ada_kernel(c_ref, w_ref, b_ref, o_ref):
    c = c_ref[...]
    s = (c * jax.nn.sigmoid(c)).astype(BF16)
    o_ref[...] = _dot(s, w_ref[...].astype(BF16)) + b_ref[...]


def _ada_mods(c_all, w, b):
    nl, d, n = w.shape
    r = c_all.shape[0]
    tn = 1024
    return pl.pallas_call(
        _ada_kernel,
        grid=(nl, n // tn),
        in_specs=[pl.BlockSpec((r, d), lambda l, j: (0, 0)),
                  pl.BlockSpec((None, d, tn), lambda l, j: (l, 0, j)),
                  pl.BlockSpec((None, 1, tn), lambda l, j: (l, 0, j))],
        out_specs=pl.BlockSpec((None, r, tn), lambda l, j: (l, 0, j)),
        out_shape=jax.ShapeDtypeStruct((nl, r, n), F32),
        compiler_params=_params(2),
        name="ada_mods",
    )(c_all, w, b.reshape(nl, 1, n))


class _Mods:
    def __init__(self, arr, d, db):
        self.a3 = arr
        self.a4 = arr.reshape(arr.shape[0], arr.shape[1], 1, arr.shape[2])
        self.d = d
        self.db = db

    def spec_rows(self, l, k, row_block, n_rows):
        return self.a3, pl.BlockSpec((None, n_rows, self.d), lambda *_: (l, row_block, k))

    def spec_prompt_row(self, l, k):
        db = self.db
        return self.a4, pl.BlockSpec((None, None, 1, self.d), lambda b, *_: (l, db + b, 0, k))


def _disc_kernel(lr_ref, li_ref, ls_ref, br_ref, bi_ref, ar_ref, ai_ref, bbr_ref, bbi_ref):
    dt = jnp.exp(ls_ref[...])
    lr, li = lr_ref[...], li_ref[...]
    mag = jnp.exp(lr * dt)
    a_re, a_im = mag * jnp.cos(li * dt), mag * jnp.sin(li * dt)
    nr, ni = a_re - 1.0, a_im
    den = lr * lr + li * li
    f_re = (nr * lr + ni * li) / den
    f_im = (ni * lr - nr * li) / den
    br, bi = br_ref[...], bi_ref[...]
    ar_ref[...] = a_re
    ai_ref[...] = a_im
    bbr_ref[...] = f_re * br - f_im * bi
    bbi_ref[...] = f_re * bi + f_im * br


def _discretize(lam_re, lam_im, log_step, b_re, b_im):
    na, g, n = lam_re.shape
    p = b_re.shape[-1]
    gn = g * n
    vec = lambda a: a.reshape(na, 1, gn)
    ls = jnp.broadcast_to(log_step[:, :, None], (na, g, n)).reshape(na, 1, gn)
    bt = lambda a: a.transpose(0, 3, 1, 2).reshape(na, p, gn)
    row = pl.BlockSpec((None, 1, gn), lambda a: (a, 0, 0))
    mat = pl.BlockSpec((None, p, gn), lambda a: (a, 0, 0))
    return pl.pallas_call(
        _disc_kernel,
        grid=(na,),
        in_specs=[row, row, row, mat, mat],
        out_specs=[row, row, mat, mat],
        out_shape=[jax.ShapeDtypeStruct((na, 1, gn), F32)] * 2
        + [jax.ShapeDtypeStruct((na, p, gn), F32)] * 2,
        compiler_params=_params(1),
        name="s5_discretize",
    )(vec(lam_re), vec(lam_im), ls, bt(b_re), bt(b_im))


def _block_diag_in(bb_re, bb_im, g, n):
    na, p, _ = bb_re.shape
    gb = GROUPS_PER_BLOCK
    nj = g // gb
    eye = jnp.eye(gb, dtype=F32)

    def one(bb):
        b5 = bb.reshape(na, p, nj, gb, n)
        w = jnp.einsum('apjgn,gh->ajgphn', b5, eye)
        return w.reshape(na, nj, gb * p, gb * n)

    return jnp.concatenate([one(bb_re), one(bb_im)], axis=-1).astype(BF16)


def _block_diag_out(c_re, c_im):
    na, g, p, n = c_re.shape
    gb = GROUPS_PER_BLOCK
    nj = g // gb
    eye = jnp.eye(gb, dtype=F32)

    def one(c):
        c5 = c.reshape(na, nj, gb, p, n)
        w = jnp.einsum('ajgpn,gh->ajgnhp', c5, eye)
        return w.reshape(na, nj, gb * n, gb * p)

    return jnp.concatenate([one(c_re), one(-c_im)], axis=2).astype(BF16)


def _s5_kernel(x_ref, g_ref, sh_ref, sc_ref, gt_ref, h0r_ref, h0i_ref, ar_ref, ai_ref,
               wb_ref, wc_ref, dsk_ref, wglu_ref,
               xo_ref, hfr_ref, hfi_ref,
               str_ref, sti_ref, bu_ref, y_ref, *, col_w):
    tc, nb, d = x_ref.shape
    rows = tc * nb
    nj = wb_ref.shape[0]
    half = wb_ref.shape[2] // 2

    @pl.when(pl.program_id(0) == 0)
    def _():
        str_ref[...] = h0r_ref[...]
        sti_ref[...] = h0i_ref[...]

    x = x_ref[...]
    u = _modulate(x, g_ref[...], sh_ref[...], sc_ref[...]).reshape(rows, d)
    ub = u.astype(BF16)

    for j in range(nj):
        kin = wb_ref.shape[1]
        bu_ref[...] = _dot(ub[:, j * kin:(j + 1) * kin], wb_ref[j])
        for cs in range(0, half, col_w):
            g0 = j * half + cs
            ar = jnp.broadcast_to(ar_ref[:, g0:g0 + col_w], (nb, col_w))
            ai = jnp.broadcast_to(ai_ref[:, g0:g0 + col_w], (nb, col_w))

            def step(t, carry, cs=cs, ar=ar, ai=ai):
                hr, hi = carry
                r0 = pl.multiple_of(t * nb, nb)
                bur = bu_ref[pl.ds(r0, nb), cs:cs + col_w]
                bui = bu_ref[pl.ds(r0, nb), half + cs:half + cs + col_w]
                nhr = ar * hr - ai * hi + bur
                nhi = ar * hi + ai * hr + bui
                bu_ref[pl.ds(r0, nb), cs:cs + col_w] = nhr
                bu_ref[pl.ds(r0, nb), half + cs:half + cs + col_w] = nhi
                return nhr, nhi

            hr, hi = lax.fori_loop(0, tc, step,
                                   (str_ref[:, g0:g0 + col_w], sti_ref[:, g0:g0 + col_w]),
                                   unroll=min(tc, 8))
            str_ref[:, g0:g0 + col_w] = hr
            sti_ref[:, g0:g0 + col_w] = hi
        kout = wc_ref.shape[2]
        y_ref[:, j * kout:(j + 1) * kout] = _dot(bu_ref[...].astype(BF16), wc_ref[j])

    y = y_ref[...] + dsk_ref[...] * u
    z = _dot(jax.nn.gelu(y).astype(BF16), wglu_ref[...])
    out = z[:, :d] * jax.nn.sigmoid(z[:, d:])
    xo_ref[...] = x + gt_ref[...] * out.reshape(tc, nb, d)
    hfr_ref[...] = str_ref[...]
    hfi_ref[...] = sti_ref[...]


def _s5_layer(x_tm, mods, l, row_block, norm_g, h0_re, h0_im, a_re, a_im, wb, wc, dskip, wglu, tc):
    t, nb, d = x_tm.shape
    gn = a_re.shape[-1]
    rows = tc * nb
    col_w = max(LANES, (8 * 512) // nb)
    xspec = pl.BlockSpec((tc, nb, d), lambda c: (c, 0, 0))
    (m_arr, sh), (_, sc), (_, gt) = (mods.spec_rows(l, k, row_block, nb) for k in range(3))
    st_spec = pl.BlockSpec((nb, gn), lambda c: (0, 0))
    return pl.pallas_call(
        functools.partial(_s5_kernel, col_w=col_w),
        grid=(t // tc,),
        in_specs=[xspec, _const_spec((1, d)), sh, sc, gt, st_spec, st_spec,
                  _const_spec((1, gn)), _const_spec((1, gn)),
                  _const_spec(wb.shape), _const_spec(wc.shape),
                  _const_spec((1, d)), _const_spec(wglu.shape)],
        out_specs=[xspec, st_spec, st_spec],
        out_shape=[jax.ShapeDtypeStruct(x_tm.shape, F32),
                   jax.ShapeDtypeStruct((nb, gn), F32), jax.ShapeDtypeStruct((nb, gn), F32)],
        scratch_shapes=[pltpu.VMEM((nb, gn), F32), pltpu.VMEM((nb, gn), F32),
                        pltpu.VMEM((rows, wb.shape[2]), F32), pltpu.VMEM((rows, d), F32)],
        compiler_params=_params(1),
        name="s5_layer",
    )(x_tm, norm_g, m_arr, m_arr, m_arr, h0_re, h0_im, a_re, a_im, wb, wc, dskip, wglu)


class _Tokens:
    def __init__(self, kind, nb, t, d, tt=None):
        self.kind, self.nb, self.t, self.d, self.tt = kind, nb, t, d, tt

    @property
    def grid(self):
        return (self.nb, self.t // self.tt) if self.kind != "slab" else (1,)

    def view(self, x, width=None):
        w = self.d if width is None else width
        if self.kind == "slab":
            return x, pl.BlockSpec((self.t, self.nb, w), lambda i: (0, 0, 0))
        if self.kind == "tm":
            return x.reshape(self.t, self.nb * w), pl.BlockSpec((self.tt, w), lambda b, i: (i, b))
        return x, pl.BlockSpec((None, self.tt, w), lambda b, i: (b, i, 0))

    def out_shape(self, dtype=F32, width=None, kind=None):
        w = self.d if width is None else width
        kind = self.kind if kind is None else kind
        if kind == "slab":
            return jax.ShapeDtypeStruct((self.t, self.nb, w), dtype)
        if kind == "tm":
            return jax.ShapeDtypeStruct((self.t, self.nb * w), dtype)
        return jax.ShapeDtypeStruct((self.nb, self.t, w), dtype)

    def out_spec(self, width=None, kind=None):
        w = self.d if width is None else width
        kind = self.kind if kind is None else kind
        if kind == "slab":
            return pl.BlockSpec((self.t, self.nb, w), lambda i: (0, 0, 0))
        if kind == "tm":
            return pl.BlockSpec((self.tt, w), lambda b, i: (i, b))
        return pl.BlockSpec((None, self.tt, w), lambda b, i: (b, i, 0))

    def mods(self, mods, l, n):
        if self.kind == "slab":
            return [mods.spec_rows(l, k, 0, self.nb) for k in range(n)]
        return [mods.spec_prompt_row(l, k) for k in range(n)]


def _ffn_chunks(f):
    out, c0 = [], 0
    while c0 < f:
        c1 = min(f, c0 + 1024)
        out.append((c0, c1))
        c0 = c1
    return out


def _ffn_kernel(x_ref, g_ref, sh_ref, sc_ref, gt_ref, w13_ref, w2_ref, o_ref):
    x = x_ref[...]
    d = x.shape[-1]
    f = w2_ref.shape[0]
    h = _modulate(x, g_ref[...], sh_ref[...], sc_ref[...]).reshape(-1, d).astype(BF16)
    acc = None
    for c0, c1 in _ffn_chunks(f):
        a = _dot(h, w13_ref[:, c0:c1])
        b = _dot(h, w13_ref[:, f + c0:f + c1])
        gte = (a * jax.nn.sigmoid(a) * b).astype(BF16)
        part = _dot(gte, w2_ref[c0:c1, :])
        acc = part if acc is None else acc + part
    o_ref[...] = x + gt_ref[...] * acc.reshape(x.shape)


def _ffn_layer(x, tok, mods, l, norm_g, w13, w2, out_kind=None):
    xv, xspec = tok.view(x)
    (m_arr, sh), (_, sc), (_, gt) = tok.mods(mods, l, 3)
    return pl.pallas_call(
        _ffn_kernel,
        grid=tok.grid,
        in_specs=[xspec, _const_spec((1, tok.d)), sh, sc, gt,
                  _const_spec(w13.shape), _const_spec(w2.shape)],
        out_specs=tok.out_spec(kind=out_kind),
        out_shape=tok.out_shape(kind=out_kind),
        compiler_params=_params(len(tok.grid)),
        name="swiglu_ffn",
    )(xv, norm_g, m_arr, m_arr, m_arr, w13, w2)


def _kv_kernel(x_ref, g_ref, sh_ref, sc_ref, wkv_ref, ones_ref, knw_ref,
               k_ref, v_ref, kb_ref, vb_ref):
    x = x_ref[...]
    d = x.shape[-1]
    h = _modulate(x, g_ref[...], sh_ref[...], sc_ref[...]).reshape(-1, d).astype(BF16)
    kv = _dot(h, wkv_ref[...])
    nk = k_ref.shape[-1]
    k = _group_rms(kv[:, :nk], ones_ref[...], knw_ref[...])
    v = kv[:, nk:]
    k_ref[...] = k.reshape(k_ref.shape)
    v_ref[...] = v.reshape(v_ref.shape)
    kb_ref[...] = k.astype(BF16).reshape(kb_ref.shape)
    vb_ref[...] = v.astype(BF16).reshape(vb_ref.shape)


def _kv_layer(x, tok, mods, norm_g, wkv, ones_blk, knw, out_kind):
    xv, xspec = tok.view(x)
    (m_arr, sh), (_, sc) = tok.mods(mods, 0, 2)
    nk = knw.shape[-1]
    nv = wkv.shape[1] - nk
    widths = (nk, nv, nk, nv)
    dtypes = (F32, F32, BF16, BF16)
    return pl.pallas_call(
        _kv_kernel,
        grid=tok.grid,
        in_specs=[xspec, _const_spec((1, tok.d)), sh, sc, _const_spec(wkv.shape),
                  _const_spec(ones_blk.shape), _const_spec(knw.shape)],
        out_specs=[tok.out_spec(width=w, kind=out_kind) for w in widths],
        out_shape=[tok.out_shape(dtype=t, width=w, kind=out_kind) for w, t in zip(widths, dtypes)],
        compiler_params=_params(len(tok.grid)),
        name="shared_kv",
    )(xv, norm_g, m_arr, m_arr, wkv, ones_blk, knw)


def _diff_lambda(lp_ref, lam_init):
    lp = lp_ref[...]
    return (jnp.exp(jnp.sum(lp[0:1] * lp[1:2], axis=-1, keepdims=True))
            - jnp.exp(jnp.sum(lp[2:3] * lp[3:4], axis=-1, keepdims=True)) + lam_init)


def _query(x, g, sh, sc, wq, ones_blk, qnw):
    h = _modulate(x, g, sh, sc).reshape(-1, x.shape[-1]).astype(BF16)
    q = _group_rms(_dot(h, wq), ones_blk, qnw)
    return q * ATTN_SCALE


def _head_out(o, subw, lam_init):
    ms = jnp.mean(o * o, axis=-1, keepdims=True)
    return o * lax.rsqrt(ms + EPS) * subw * (1.0 - lam_init)


def _attn_prompt_kernel(x_ref, g_ref, sh_ref, sc_ref, gt_ref, wq_ref, ones_ref, qnw_ref,
                        kb_ref, vb_ref, lp_ref, subw_ref, wo_ref,
                        o_ref,
                        q0_ref, q1_ref, m_ref, l_ref, acc_ref, *, lam_init):
    tq, d = x_ref.shape
    tk = tq
    i = pl.program_id(1)
    x = x_ref[...]
    q = _query(x, g_ref[...], sh_ref[...], sc_ref[...], wq_ref[...], ones_ref[...], qnw_ref[...])
    lane = lax.broadcasted_iota(jnp.int32, q.shape, 1)
    first_map = (lane % HEAD_DV) < HEAD_DK
    q0_ref[...] = jnp.where(first_map, q, 0.0).astype(BF16)
    q1_ref[...] = jnp.where(first_map, 0.0, q).astype(BF16)
    m_ref[...] = jnp.full(m_ref.shape, NEG_BIG, F32)
    l_ref[...] = jnp.zeros(l_ref.shape, F32)
    acc_ref[...] = jnp.zeros(acc_ref.shape, F32)

    def attend(j, masked):
        k0 = pl.multiple_of(j * tk, tk)
        for h in range(N_HEADS):
            hs = slice(h * HEAD_DV, (h + 1) * HEAD_DV)
            kh = kb_ref[pl.ds(k0, tk), hs]
            vh = vb_ref[pl.ds(k0, tk), hs]
            for m, qm_ref in enumerate((q0_ref, q1_ref)):
                idx = 2 * h + m
                s = _dot_nt(qm_ref[:, hs], kh)
                if masked:
                    row = lax.broadcasted_iota(jnp.int32, s.shape, 0)
                    col = lax.broadcasted_iota(jnp.int32, s.shape, 1)
                    s = jnp.where(col <= row, s, NEG_BIG)
                m_old = m_ref[idx]
                m_new = jnp.maximum(m_old, jnp.max(s, axis=-1, keepdims=True))
                alpha = jnp.exp(m_old - m_new)
                p = jnp.exp(s - jnp.concatenate([m_new] * (tk // LANES), axis=1))
                l_ref[idx] = alpha * l_ref[idx] + jnp.sum(p, axis=-1, keepdims=True)
                acc_ref[idx] = alpha * acc_ref[idx] + _dot(p.astype(BF16), vh)
                m_ref[idx] = m_new

    def body(j, carry):
        attend(j, False)
        return carry

    lax.fori_loop(0, i, body, 0)
    attend(i, True)

    lam = _diff_lambda(lp_ref, lam_init)
    heads = []
    for h in range(N_HEADS):
        hs = slice(h * HEAD_DV, (h + 1) * HEAD_DV)
        o = acc_ref[2 * h] / l_ref[2 * h] - lam * (acc_ref[2 * h + 1] / l_ref[2 * h + 1])
        heads.append(_head_out(o, subw_ref[:, hs], lam_init))
    ob = jnp.concatenate(heads, axis=1).astype(BF16)
    o_ref[...] = x + gt_ref[...] * _dot(ob, wo_ref[...])


def _attn_prompt_layer(x, tok, mods, l, norm_g, wq, ones_blk, qnw, kb, vb, lp, subw, wo,
                       lam_init, out_kind):
    xv, xspec = tok.view(x)
    (m_arr, sh), (_, sc), (_, gt) = tok.mods(mods, l, 3)
    seq, d, tq = tok.t, tok.d, tok.tt
    kvspec = pl.BlockSpec((None, seq, d), lambda b, i: (b, 0, 0))
    stat = pltpu.VMEM((2 * N_HEADS, tq, LANES), F32)
    return pl.pallas_call(
        functools.partial(_attn_prompt_kernel, lam_init=lam_init),
        grid=tok.grid,
        in_specs=[xspec, _const_spec((1, d)), sh, sc, gt, _const_spec(wq.shape),
                  _const_spec(ones_blk.shape), _const_spec(qnw.shape), kvspec, kvspec,
                  _const_spec(lp.shape), _const_spec(subw.shape), _const_spec(wo.shape)],
        out_specs=tok.out_spec(kind=out_kind),
        out_shape=tok.out_shape(kind=out_kind),
        scratch_shapes=[pltpu.VMEM((tq, d), BF16), pltpu.VMEM((tq, d), BF16), stat, stat, stat],
        compiler_params=_params(2),
        name="diff_attn_prompt",
    )(xv, norm_g, m_arr, m_arr, m_arr, wq, ones_blk, qnw, kb, vb, lp, subw, wo)


def _query_kernel(x_ref, g_ref, sh_ref, sc_ref, wq_ref, ones_ref, qnw_ref, q_ref):
    q = _query(x_ref[...], g_ref[...], sh_ref[...], sc_ref[...], wq_ref[...], ones_ref[...],
               qnw_ref[...])
    q_ref[...] = q.astype(BF16).reshape(q_ref.shape)


def _query_layer(x, tok, mods, l, norm_g, wq, ones_blk, qnw):
    xv, xspec = tok.view(x)
    (m_arr, sh), (_, sc) = tok.mods(mods, l, 2)
    return pl.pallas_call(
        _query_kernel,
        grid=tok.grid,
        in_specs=[xspec, _const_spec((1, tok.d)), sh, sc, _const_spec(wq.shape),
                  _const_spec(ones_blk.shape), _const_spec(qnw.shape)],
        out_specs=tok.out_spec(),
        out_shape=tok.out_shape(dtype=BF16),
        compiler_params=_params(len(tok.grid)),
        name="diff_attn_query",
    )(xv, norm_g, m_arr, m_arr, wq, ones_blk, qnw)


def _attn_paged_kernel(pt_ref, q_ref, kn_ref, vn_ref, lp_ref, *rest, n_pages, lam_init):
    k_refs = rest[:n_pages]
    v_refs = rest[n_pages:2 * n_pages]
    o_ref = rest[2 * n_pages]
    qx_ref, m_ref, l_ref, acc_ref = rest[2 * n_pages + 1:]
    c = pl.program_id(1)
    t_new, d = q_ref.shape
    n_rows = 2 * N_HEADS * t_new
    page = k_refs[0].shape[0]

    @pl.when(c == 0)
    def _():
        qt = jnp.concatenate([q_ref[...].astype(F32)] * (2 * N_HEADS), axis=0)
        row = lax.broadcasted_iota(jnp.int32, qt.shape, 0)
        col = lax.broadcasted_iota(jnp.int32, qt.shape, 1)
        qx_ref[...] = jnp.where(row // t_new == col // HEAD_DK, qt, 0.0).astype(BF16)
        m_ref[...] = jnp.full(m_ref.shape, NEG_BIG, F32)
        l_ref[...] = jnp.zeros(l_ref.shape, F32)
        acc_ref[...] = jnp.zeros(acc_ref.shape, F32)

    def update(s, vb):
        m_old = m_ref[...]
        m_new = jnp.maximum(m_old, jnp.max(s, axis=-1, keepdims=True))
        alpha = jnp.exp(m_old - m_new)
        p = jnp.exp(s - jnp.concatenate([m_new] * (s.shape[1] // LANES), axis=1))
        l_ref[...] = alpha * l_ref[...] + jnp.sum(p, axis=-1, keepdims=True)
        acc_ref[...] = (jnp.concatenate([alpha] * (d // LANES), axis=1) * acc_ref[...]
                        + _dot(p.astype(BF16), vb))
        m_ref[...] = m_new

    qx = qx_ref[...]
    kb = jnp.concatenate([r[...].astype(BF16) for r in k_refs], axis=0)
    vb = jnp.concatenate([r[...].astype(BF16) for r in v_refs], axis=0)
    update(_dot_nt(qx, kb), vb)

    @pl.when(c == pl.num_programs(1) - 1)
    def _():
        pad = jnp.zeros((page - t_new, d), F32)
        kb_new = jnp.concatenate([kn_ref[...], pad], axis=0).astype(BF16)
        vb_new = jnp.concatenate([vn_ref[...], pad], axis=0).astype(BF16)
        s = _dot_nt(qx, kb_new)
        row = lax.broadcasted_iota(jnp.int32, s.shape, 0)
        col = lax.broadcasted_iota(jnp.int32, s.shape, 1)
        s = jnp.where(col <= row % t_new, s, NEG_BIG)
        update(s, vb_new)
        lam = _diff_lambda(lp_ref, lam_init)
        inv_l = 1.0 / l_ref[...]
        for h in range(N_HEADS):
            hs = slice(h * HEAD_DV, (h + 1) * HEAD_DV)
            r0 = 2 * h * t_new
            o0 = acc_ref[r0:r0 + t_new, hs] * inv_l[r0:r0 + t_new]
            o1 = acc_ref[r0 + t_new:r0 + 2 * t_new, hs] * inv_l[r0 + t_new:r0 + 2 * t_new]
            o_ref[:, hs] = o0 - lam * o1


def _attn_paged_layer(q_tm, kn_tm, vn_tm, cache_k, cache_v, page_table, lp, lam_init, n_pages):
    t_new, nb, d = q_tm.shape
    page = cache_k.shape[1]
    n_chunks = page_table.shape[1] // n_pages
    n_rows = 2 * N_HEADS * t_new
    row_spec = pl.BlockSpec((t_new, d), lambda b, c, pt: (0, b))

    def page_spec(p):
        return pl.BlockSpec((None, page, d), lambda b, c, pt: (pt[b, c * n_pages + p], 0, 0))

    pages = [page_spec(p) for p in range(n_pages)]
    grid_spec = pltpu.PrefetchScalarGridSpec(
        num_scalar_prefetch=1,
        grid=(nb, n_chunks),
        in_specs=[row_spec, row_spec, row_spec,
                  pl.BlockSpec(lp.shape, lambda b, c, pt: (0, 0))] + pages + pages,
        out_specs=row_spec,
        scratch_shapes=[pltpu.VMEM((n_rows, d), BF16), pltpu.VMEM((n_rows, LANES), F32),
                        pltpu.VMEM((n_rows, LANES), F32), pltpu.VMEM((n_rows, d), F32)],
    )
    flat = lambda a: a.reshape(t_new, nb * d)
    o = pl.pallas_call(
        functools.partial(_attn_paged_kernel, n_pages=n_pages, lam_init=lam_init),
        grid_spec=grid_spec,
        out_shape=jax.ShapeDtypeStruct((t_new, nb * d), F32),
        compiler_params=_params(2),
        name="diff_attn_paged",
    )(page_table, flat(q_tm), flat(kn_tm), flat(vn_tm), lp,
      *([cache_k] * n_pages), *([cache_v] * n_pages))
    return o.reshape(t_new, nb, d)


def _attn_out_kernel(x_ref, o_ref_in, gt_ref, subw_ref, wo_ref, out_ref, *, lam_init):
    x = x_ref[...]
    d = x.shape[-1]
    o = o_ref_in[...].reshape(-1, d)
    heads = []
    for h in range(N_HEADS):
        hs = slice(h * HEAD_DV, (h + 1) * HEAD_DV)
        heads.append(_head_out(o[:, hs], subw_ref[:, hs], lam_init))
    ob = jnp.concatenate(heads, axis=1).astype(BF16)
    out_ref[...] = x + gt_ref[...] * _dot(ob, wo_ref[...]).reshape(x.shape)


def _attn_out_layer(x, o, tok, mods, l, subw, wo, lam_init):
    xv, xspec = tok.view(x)
    ov, ospec = tok.view(o)
    m_arr, gt = tok.mods(mods, l, 3)[2]
    return pl.pallas_call(
        functools.partial(_attn_out_kernel, lam_init=lam_init),
        grid=tok.grid,
        in_specs=[xspec, ospec, gt, _const_spec(subw.shape), _const_spec(wo.shape)],
        out_specs=tok.out_spec(),
        out_shape=tok.out_shape(),
        compiler_params=_params(len(tok.grid)),
        name="diff_attn_out",
    )(xv, ov, m_arr, subw, wo)


def kernel(x_prompt, x_sample, state_ssm_re, state_ssm_im, cache_k, cache_v, page_table, c_prompt, c_sample, norm_mix, w_ada_mix, b_ada_mix, norm_ffn, w_ada_ffn, b_ada_ffn, ffn_w13, ffn_w2, ssm_lam_re, ssm_lam_im, ssm_log_step, ssm_b_re, ssm_b_im, ssm_c_re, ssm_c_im, ssm_d, ssm_w_glu, norm_kv, w_ada_kv, b_ada_kv, w_kv, k_norm, w_q, q_norm, diff_lambda, subln, w_o):
    pb, seq, d = x_prompt.shape
    db, dseq, _ = x_sample.shape
    depth = norm_mix.shape[0]
    n_a, g, n = ssm_lam_re.shape
    gn = g * n
    nk = N_HEADS * 2 * HEAD_DK

    c_all = jnp.concatenate([c_sample, c_prompt], axis=0)
    mods_mix = _Mods(_ada_mods(c_all, w_ada_mix, b_ada_mix), d, db)
    mods_ffn = _Mods(_ada_mods(c_all, w_ada_ffn, b_ada_ffn), d, db)
    mods_kv = _Mods(_ada_mods(c_all, w_ada_kv[None], b_ada_kv[None]), d, db)

    a_re, a_im, bb_re, bb_im = _discretize(ssm_lam_re, ssm_lam_im, ssm_log_step, ssm_b_re, ssm_b_im)
    wb = _block_diag_in(bb_re, bb_im, g, n)
    wc = _block_diag_out(ssm_c_re, ssm_c_im)
    w13, w2, wglu = ffn_w13.astype(BF16), ffn_w2.astype(BF16), ssm_w_glu.astype(BF16)
    wkv, wq, wo = w_kv.astype(BF16), w_q.astype(BF16), w_o.astype(BF16)
    row = lambda a: a.reshape(1, -1)
    knw = row(jnp.tile(k_norm.reshape(-1), N_HEADS))
    qnw = [row(jnp.tile(q_norm[j].reshape(-1), N_HEADS)) for j in range(depth - n_a)]
    subw = [row(jnp.tile(subln[j], N_HEADS)) for j in range(depth - n_a)]
    blk = jnp.arange(MXU_DIM) // HEAD_DK
    ones_blk = ((blk[:, None] == blk[None, :]).astype(F32) / HEAD_DK).astype(BF16)
    lam_inits = [0.8 - 0.6 * math.exp(-0.3 * layer) for layer in range(n_a, depth)]

    ck = cache_k.reshape(cache_k.shape[0], cache_k.shape[1], nk)
    cv = cache_v.reshape(cache_v.shape[0], cache_v.shape[1], N_HEADS * HEAD_DV)

    def run_group(x_tm, h0_re, h0_im, row_block, tc, toks, paged):
        tok_pre, tok_at0, tok_at, tok_post = toks
        nb = x_tm.shape[1]
        new_re, new_im = [], []
        x = x_tm
        for a in range(n_a):
            x, hr, hi = _s5_layer(x.reshape(x_tm.shape), mods_mix, a, row_block, row(norm_mix[a]),
                                  h0_re[a], h0_im[a], a_re[a], a_im[a], wb[a], wc[a], row(ssm_d[a]),
                                  wglu[a], tc)
            new_re.append(hr.reshape(nb, g, n))
            new_im.append(hi.reshape(nb, g, n))
            x = _ffn_layer(x, tok_pre, mods_ffn, a, row(norm_ffn[a]), w13[a], w2[a])
        k, v, kb, vb = _kv_layer(x, tok_pre, mods_kv, row(norm_kv), wkv, ones_blk, knw,
                                 out_kind=tok_post.kind)
        for j in range(depth - n_a):
            layer = n_a + j
            tok = tok_at0 if j == 0 else tok_at
            if paged:
                q = _query_layer(x, tok, mods_mix, layer, row(norm_mix[layer]), wq[j], ones_blk, qnw[j])
                o = _attn_paged_layer(q, k, v, ck, cv, page_table, diff_lambda[j], lam_inits[j], 8)
                x = _attn_out_layer(x, o, tok, mods_mix, layer, subw[j], wo[j], lam_inits[j])
            else:
                x = _attn_prompt_layer(x, tok, mods_mix, layer, row(norm_mix[layer]), wq[j], ones_blk,
                                       qnw[j], kb, vb, diff_lambda[j], subw[j], wo[j], lam_inits[j],
                                       out_kind=tok_post.kind)
            x = _ffn_layer(x, tok_post, mods_ffn, layer, row(norm_ffn[layer]), w13[layer], w2[layer])
        return x, jnp.stack(new_re), jnp.stack(new_im), k, v

    zeros = jnp.zeros((n_a, pb, gn), F32)
    toks_p = (_Tokens("tm", pb, seq, d, tt=512), _Tokens("tm", pb, seq, d, tt=256),
              _Tokens("bm", pb, seq, d, tt=256), _Tokens("bm", pb, seq, d, tt=512))
    y_p, re_p, im_p, k_p, v_p = run_group(
        x_prompt.transpose(1, 0, 2), zeros, zeros, db // pb, 64, toks_p, False)

    tok_s = _Tokens("slab", db, dseq, d)
    y_s, re_s, im_s, k_s, v_s = run_group(
        x_sample.transpose(1, 0, 2), state_ssm_re.reshape(n_a, db, gn).astype(F32),
        state_ssm_im.reshape(n_a, db, gn).astype(F32), 0, dseq, (tok_s,) * 4, True)
    tm2bm = lambda a: a.transpose(1, 0, 2)

    return (y_p, tm2bm(y_s), re_p, im_p, re_s, im_s,
            k_p.reshape(pb, seq, N_HEADS, 2, HEAD_DK), v_p.reshape(pb, seq, N_HEADS, HEAD_DV),
            tm2bm(k_s).reshape(db, dseq, N_HEADS, 2, HEAD_DK), tm2bm(v_s).reshape(db, dseq, N_HEADS, HEAD_DV))
```

```python
import functools
import math

import jax
import jax.numpy as jnp
from jax import lax
from jax.experimental import pallas as pl
from jax.experimental.pallas import tpu as pltpu

F32 = jnp.float32
BF16 = jnp.bfloat16

EPS = 1e-6
N_HEADS = 8
HEAD_DK = 64
HEAD_DV = 2 * HEAD_DK
ATTN_SCALE = HEAD_DK ** -0.5
SSM_GROUP = 16
SSM_STATE = 64
LANES = 128
MXU_DIM = 256
GROUPS_PER_BLOCK = MXU_DIM // SSM_GROUP
VMEM_LIMIT = 56 * 1024 * 1024
NEG_BIG = -1e30
LOG2_E = math.log2(math.e)


def _params(n_grid, vmem=VMEM_LIMIT):
    return pltpu.CompilerParams(dimension_semantics=("arbitrary",) * n_grid,
                                vmem_limit_bytes=vmem)


def _const_spec(shape):
    nd = len(shape)
    return pl.BlockSpec(tuple(shape), lambda *_: (0,) * nd, pipeline_mode=pl.Buffered(1))


def _layer_spec(stacked, l):
    nd = stacked.ndim - 1
    return pl.BlockSpec((None,) + stacked.shape[1:], lambda *_: (l,) + (0,) * nd,
                        pipeline_mode=pl.Buffered(1))


def _dot(a, b):
    return jnp.dot(a, b, preferred_element_type=F32)


def _dot_nt(a, b):
    return lax.dot_general(a, b, (((1,), (1,)), ((), ())), preferred_element_type=F32)


def _modulate(x, g, shift, scale):
    ms = jnp.mean(x * x, axis=-1, keepdims=True)
    xn = x * lax.rsqrt(ms + EPS) * g
    return xn * (1.0 + scale) + shift


def _group_rms(y, ones_blk, w):
    parts = []
    for j in range(y.shape[1] // MXU_DIM):
        blk = y[:, j * MXU_DIM:(j + 1) * MXU_DIM]
        parts.append(_dot((blk * blk).astype(BF16), ones_blk))
    ms = jnp.concatenate(parts, axis=1)
    return y * lax.rsqrt(ms + EPS) * w


def _ada_kernel(c_ref, w_ref, b_ref, o_ref):
    c = c_ref[...]
    s = (c * jax.nn.sigmoid(c)).astype(BF16)
    o_ref[...] = _dot(s, w_ref[...].astype(BF16)) + b_ref[...]


def _ada_mods(c_all, w, b):
    nl, d, n = w.shape
    r = c_all.shape[0]
    tn = 1024
    return pl.pallas_call(
        _ada_kernel,
        grid=(nl, n // tn),
        in_specs=[pl.BlockSpec((r, d), lambda l, j: (0, 0)),
                  pl.BlockSpec((None, d, tn), lambda l, j: (l, 0, j)),
                  pl.BlockSpec((None, 1, tn), lambda l, j: (l, 0, j))],
        out_specs=pl.BlockSpec((None, r, tn), lambda l, j: (l, 0, j)),
        out_shape=jax.ShapeDtypeStruct((nl, r, n), F32),
        compiler_params=_params(2),
        name="ada_mods",
    )(c_all, w, b.reshape(nl, 1, n))


class _Mods:
    def __init__(self, arr, d, db):
        self.a3 = arr
        self.a4 = arr.reshape(arr.shape[0], arr.shape[1], 1, arr.shape[2])
        self.d = d
        self.db = db

    def spec_rows(self, l, k, row_block, n_rows):
        return self.a3, pl.BlockSpec((None, n_rows, self.d), lambda *_: (l, row_block, k))

    def spec_prompt_row(self, l, k):
        db = self.db
        return self.a4, pl.BlockSpec((None, None, 1, self.d), lambda b, *_: (l, db + b, 0, k))


def _disc_kernel(lr_ref, li_ref, ls_ref, br_ref, bi_ref, ar_ref, ai_ref, bbr_ref, bbi_ref):
    dt = jnp.exp(ls_ref[...])
    lr, li = lr_ref[...], li_ref[...]
    mag = jnp.exp(lr * dt)
    a_re, a_im = mag * jnp.cos(li * dt), mag * jnp.sin(li * dt)
    nr, ni = a_re - 1.0, a_im
    den = lr * lr + li * li
    f_re = (nr * lr + ni * li) / den
    f_im = (ni * lr - nr * li) / den
    br, bi = br_ref[...], bi_ref[...]
    ar_ref[...] = a_re
    ai_ref[...] = a_im
    bbr_ref[...] = f_re * br - f_im * bi
    bbi_ref[...] = f_re * bi + f_im * br


def _discretize(lam_re, lam_im, log_step, b_re, b_im):
    na, g, n = lam_re.shape
    p = b_re.shape[-1]
    gn = g * n
    vec = lambda a: a.reshape(na, 1, gn)
    ls = jnp.broadcast_to(log_step[:, :, None], (na, g, n)).reshape(na, 1, gn)
    bt = lambda a: a.transpose(0, 3, 1, 2).reshape(na, p, gn)
    row = pl.BlockSpec((None, 1, gn), lambda a: (a, 0, 0))
    mat = pl.BlockSpec((None, p, gn), lambda a: (a, 0, 0))
    return pl.pallas_call(
        _disc_kernel,
        grid=(na,),
        in_specs=[row, row, row, mat, mat],
        out_specs=[row, row, mat, mat],
        out_shape=[jax.ShapeDtypeStruct((na, 1, gn), F32)] * 2
        + [jax.ShapeDtypeStruct((na, p, gn), F32)] * 2,
        compiler_params=_params(1),
        name="s5_discretize",
    )(vec(lam_re), vec(lam_im), ls, bt(b_re), bt(b_im))


def _block_diag_in(bb_re, bb_im, g, n):
    na, p, _ = bb_re.shape
    gb = GROUPS_PER_BLOCK
    nj = g // gb
    eye = jnp.eye(gb, dtype=F32)

    def one(bb):
        b5 = bb.reshape(na, p, nj, gb, n)
        w = jnp.einsum('apjgn,gh->ajgphn', b5, eye)
        return w.reshape(na, nj, gb * p, gb * n)

    return jnp.concatenate([one(bb_re), one(bb_im)], axis=-1).astype(BF16)


def _block_diag_out(c_re, c_im):
    na, g, p, n = c_re.shape
    gb = GROUPS_PER_BLOCK
    nj = g // gb
    eye = jnp.eye(gb, dtype=F32)

    def one(c):
        c5 = c.reshape(na, nj, gb, p, n)
        w = jnp.einsum('ajgpn,gh->ajgnhp', c5, eye)
        return w.reshape(na, nj, gb * n, gb * p)

    return jnp.concatenate([one(c_re), one(-c_im)], axis=2).astype(BF16)


def _s5_kernel(x_ref, g_ref, sh_ref, sc_ref, gt_ref, h0r_ref, h0i_ref, ar_ref, ai_ref,
               wb_ref, wc_ref, dsk_ref, wglu_ref,
               xo_ref, hfr_ref, hfi_ref,
               str_ref, sti_ref, bu_ref, y_ref, *, col_w):
    tc, nb, d = x_ref.shape
    rows = tc * nb
    nj = wb_ref.shape[0]
    half = wb_ref.shape[2] // 2

    @pl.when(pl.program_id(0) == 0)
    def _():
        str_ref[...] = h0r_ref[...]
        sti_ref[...] = h0i_ref[...]

    x = x_ref[...]
    u = _modulate(x, g_ref[...], sh_ref[...], sc_ref[...]).reshape(rows, d)
    ub = u.astype(BF16)

    for j in range(nj):
        kin = wb_ref.shape[1]
        bu_ref[...] = _dot(ub[:, j * kin:(j + 1) * kin], wb_ref[j])
        for cs in range(0, half, col_w):
            g0 = j * half + cs
            ar = jnp.broadcast_to(ar_ref[:, g0:g0 + col_w], (nb, col_w))
            ai = jnp.broadcast_to(ai_ref[:, g0:g0 + col_w], (nb, col_w))

            def step(t, carry, cs=cs, ar=ar, ai=ai):
                hr, hi = carry
                r0 = pl.multiple_of(t * nb, nb)
                bur = bu_ref[pl.ds(r0, nb), cs:cs + col_w]
                bui = bu_ref[pl.ds(r0, nb), half + cs:half + cs + col_w]
                nhr = ar * hr - ai * hi + bur
                nhi = ar * hi + ai * hr + bui
                bu_ref[pl.ds(r0, nb), cs:cs + col_w] = nhr
                bu_ref[pl.ds(r0, nb), half + cs:half + cs + col_w] = nhi
                return nhr, nhi

            hr, hi = lax.fori_loop(0, tc, step,
                                   (str_ref[:, g0:g0 + col_w], sti_ref[:, g0:g0 + col_w]),
                                   unroll=True)
            str_ref[:, g0:g0 + col_w] = hr
            sti_ref[:, g0:g0 + col_w] = hi
        kout = wc_ref.shape[2]
        y_ref[:, j * kout:(j + 1) * kout] = _dot(bu_ref[...].astype(BF16), wc_ref[j])

    y = y_ref[...] + dsk_ref[...] * u
    z = _dot(jax.nn.gelu(y).astype(BF16), wglu_ref[...])
    out = z[:, :d] * jax.nn.sigmoid(z[:, d:])
    xo_ref[...] = x + gt_ref[...] * out.reshape(tc, nb, d)
    hfr_ref[...] = str_ref[...]
    hfi_ref[...] = sti_ref[...]


def _s5_layer(x_tm, mods, l, row_block, norm_g, h0_re, h0_im, a_re, a_im, wb, wc, dskip, wglu, tc):
    t, nb, d = x_tm.shape
    gn = a_re.shape[-1]
    rows = tc * nb
    col_w = max(LANES, (8 * 512) // nb)
    xspec = pl.BlockSpec((tc, nb, d), lambda c: (c, 0, 0))
    (m_arr, sh), (_, sc), (_, gt) = (mods.spec_rows(l, k, row_block, nb) for k in range(3))
    st_spec = pl.BlockSpec((nb, gn), lambda c: (0, 0))
    return pl.pallas_call(
        functools.partial(_s5_kernel, col_w=col_w),
        grid=(t // tc,),
        in_specs=[xspec, _const_spec((1, d)), sh, sc, gt, st_spec, st_spec,
                  _const_spec((1, gn)), _const_spec((1, gn)),
                  _layer_spec(wb, l), _layer_spec(wc, l),
                  _const_spec((1, d)), _layer_spec(wglu, l)],
        out_specs=[xspec, st_spec, st_spec],
        out_shape=[jax.ShapeDtypeStruct(x_tm.shape, F32),
                   jax.ShapeDtypeStruct((nb, gn), F32), jax.ShapeDtypeStruct((nb, gn), F32)],
        scratch_shapes=[pltpu.VMEM((nb, gn), F32), pltpu.VMEM((nb, gn), F32),
                        pltpu.VMEM((rows, wb.shape[3]), F32), pltpu.VMEM((rows, d), F32)],
        compiler_params=_params(1),
        name="s5_layer",
    )(x_tm, norm_g, m_arr, m_arr, m_arr, h0_re, h0_im, a_re, a_im, wb, wc, dskip, wglu)


class _Tokens:
    def __init__(self, kind, nb, t, d, tt=None):
        self.kind, self.nb, self.t, self.d, self.tt = kind, nb, t, d, tt

    @property
    def grid(self):
        return (self.nb, self.t // self.tt) if self.kind != "slab" else (1,)

    def view(self, x, width=None):
        w = self.d if width is None else width
        if self.kind == "slab":
            return x, pl.BlockSpec((self.t, self.nb, w), lambda i: (0, 0, 0))
        if self.kind == "tm":
            return x.reshape(self.t, self.nb * w), pl.BlockSpec((self.tt, w), lambda b, i: (i, b))
        return x, pl.BlockSpec((None, self.tt, w), lambda b, i: (b, i, 0))

    def out_shape(self, dtype=F32, width=None, kind=None):
        w = self.d if width is None else width
        kind = self.kind if kind is None else kind
        if kind == "slab":
            return jax.ShapeDtypeStruct((self.t, self.nb, w), dtype)
        if kind == "tm":
            return jax.ShapeDtypeStruct((self.t, self.nb * w), dtype)
        return jax.ShapeDtypeStruct((self.nb, self.t, w), dtype)

    def out_spec(self, width=None, kind=None):
        w = self.d if width is None else width
        kind = self.kind if kind is None else kind
        if kind == "slab":
            return pl.BlockSpec((self.t, self.nb, w), lambda i: (0, 0, 0))
        if kind == "tm":
            return pl.BlockSpec((self.tt, w), lambda b, i: (i, b))
        return pl.BlockSpec((None, self.tt, w), lambda b, i: (b, i, 0))

    def mods(self, mods, l, n):
        if self.kind == "slab":
            return [mods.spec_rows(l, k, 0, self.nb) for k in range(n)]
        return [mods.spec_prompt_row(l, k) for k in range(n)]


def _ffn_chunks(f):
    out, c0 = [], 0
    while c0 < f:
        c1 = min(f, c0 + 1024)
        out.append((c0, c1))
        c0 = c1
    return out


def _ffn_kernel(x_ref, g_ref, sh_ref, sc_ref, gt_ref, w13_ref, w2_ref, o_ref):
    x = x_ref[...]
    d = x.shape[-1]
    f = w2_ref.shape[0]
    h = _modulate(x, g_ref[...], sh_ref[...], sc_ref[...]).reshape(-1, d).astype(BF16)
    acc = None
    for c0, c1 in _ffn_chunks(f):
        a = _dot(h, w13_ref[:, c0:c1])
        b = _dot(h, w13_ref[:, f + c0:f + c1])
        gte = (a * jax.nn.sigmoid(a) * b).astype(BF16)
        part = _dot(gte, w2_ref[c0:c1, :])
        acc = part if acc is None else acc + part
    o_ref[...] = x + gt_ref[...] * acc.reshape(x.shape)


def _ffn_layer(x, tok, mods, l, norm_g, w13, w2, out_kind=None):
    xv, xspec = tok.view(x)
    (m_arr, sh), (_, sc), (_, gt) = tok.mods(mods, l, 3)
    return pl.pallas_call(
        _ffn_kernel,
        grid=tok.grid,
        in_specs=[xspec, _const_spec((1, tok.d)), sh, sc, gt,
                  _layer_spec(w13, l), _layer_spec(w2, l)],
        out_specs=tok.out_spec(kind=out_kind),
        out_shape=tok.out_shape(kind=out_kind),
        compiler_params=_params(len(tok.grid)),
        name="swiglu_ffn",
    )(xv, norm_g, m_arr, m_arr, m_arr, w13, w2)


def _kv_kernel(x_ref, g_ref, sh_ref, sc_ref, wkv_ref, ones_ref, knw_ref,
               k_ref, v_ref, kb_ref, vb_ref):
    x = x_ref[...]
    d = x.shape[-1]
    h = _modulate(x, g_ref[...], sh_ref[...], sc_ref[...]).reshape(-1, d).astype(BF16)
    kv = _dot(h, wkv_ref[...])
    nk = k_ref.shape[-1]
    k = _group_rms(kv[:, :nk], ones_ref[...], knw_ref[...])
    v = kv[:, nk:]
    k_ref[...] = k.reshape(k_ref.shape)
    v_ref[...] = v.reshape(v_ref.shape)
    kb_ref[...] = k.astype(BF16).reshape(kb_ref.shape)
    vb_ref[...] = v.astype(BF16).reshape(vb_ref.shape)


def _kv_layer(x, tok, mods, norm_g, wkv, ones_blk, knw, out_kind):
    xv, xspec = tok.view(x)
    (m_arr, sh), (_, sc) = tok.mods(mods, 0, 2)
    nk = knw.shape[-1]
    nv = wkv.shape[1] - nk
    widths = (nk, nv, nk, nv)
    dtypes = (F32, F32, BF16, BF16)
    return pl.pallas_call(
        _kv_kernel,
        grid=tok.grid,
        in_specs=[xspec, _const_spec((1, tok.d)), sh, sc, _const_spec(wkv.shape),
                  _const_spec(ones_blk.shape), _const_spec(knw.shape)],
        out_specs=[tok.out_spec(width=w, kind=out_kind) for w in widths],
        out_shape=[tok.out_shape(dtype=t, width=w, kind=out_kind) for w, t in zip(widths, dtypes)],
        compiler_params=_params(len(tok.grid)),
        name="shared_kv",
    )(xv, norm_g, m_arr, m_arr, wkv, ones_blk, knw)


def _diff_lambda(lp_ref, lam_init):
    lp = lp_ref[...]
    return (jnp.exp(jnp.sum(lp[0:1] * lp[1:2], axis=-1, keepdims=True))
            - jnp.exp(jnp.sum(lp[2:3] * lp[3:4], axis=-1, keepdims=True)) + lam_init)


def _query(x, g, sh, sc, wq, ones_blk, qnw):
    h = _modulate(x, g, sh, sc).reshape(-1, x.shape[-1]).astype(BF16)
    q = _group_rms(_dot(h, wq), ones_blk, qnw)
    return q * ATTN_SCALE


def _head_out(o, subw, lam_init):
    ms = jnp.mean(o * o, axis=-1, keepdims=True)
    return o * lax.rsqrt(ms + EPS) * subw * (1.0 - lam_init)


def _attn_prompt_kernel(x_ref, g_ref, sh_ref, sc_ref, gt_ref, wq_ref, ones_ref, qnw_ref,
                        kb_ref, vb_ref, lp_ref, subw_ref, wo_ref,
                        o_ref,
                        qs_ref, m_ref, acc_ref, *, lam_init):
    tq, d = x_ref.shape
    tk = tq
    i = pl.program_id(1)
    x = x_ref[...]
    q = _query(x, g_ref[...], sh_ref[...], sc_ref[...], wq_ref[...], ones_ref[...], qnw_ref[...])
    q = q * LOG2_E
    lane = lax.broadcasted_iota(jnp.int32, (tq, HEAD_DV), 1)
    first_map = lane < HEAD_DK
    for h in range(N_HEADS):
        qh = q[:, h * HEAD_DV:(h + 1) * HEAD_DV]
        qs_ref[h, :tq, :] = jnp.where(first_map, qh, 0.0).astype(BF16)
        qs_ref[h, tq:, :] = jnp.where(first_map, 0.0, qh).astype(BF16)
    m_ref[...] = jnp.full(m_ref.shape, NEG_BIG, F32)
    acc_ref[...] = jnp.zeros(acc_ref.shape, F32)

    def attend(j, masked):
        k0 = pl.multiple_of(j * tk, tk)
        for h in range(N_HEADS):
            hs = slice(h * HEAD_DV, (h + 1) * HEAD_DV)
            kh = kb_ref[pl.ds(k0, tk), hs]
            vh = vb_ref[pl.ds(k0, tk), hs]
            v_ext = jnp.concatenate([vh, jnp.ones_like(vh)], axis=1)
            s = _dot_nt(qs_ref[h], kh)
            if masked:
                row = lax.broadcasted_iota(jnp.int32, s.shape, 0)
                col = lax.broadcasted_iota(jnp.int32, s.shape, 1)
                s = jnp.where(col <= row % tq, s, NEG_BIG)
            m_old = m_ref[h]
            m_new = jnp.maximum(m_old, jnp.max(s, axis=-1, keepdims=True))
            alpha = jnp.exp2(m_old - m_new)
            p = jnp.exp2(s - jnp.concatenate([m_new] * (tk // LANES), axis=1))
            acc_ref[h] = (jnp.concatenate([alpha, alpha], axis=1) * acc_ref[h]
                          + _dot(p.astype(BF16), v_ext))
            m_ref[h] = m_new

    def body(j, carry):
        attend(j, False)
        return carry

    lax.fori_loop(0, i, body, 0)
    attend(i, True)

    lam = _diff_lambda(lp_ref, lam_init)
    heads = []
    for h in range(N_HEADS):
        hs = slice(h * HEAD_DV, (h + 1) * HEAD_DV)
        o0 = acc_ref[h, :tq, :HEAD_DV] / acc_ref[h, :tq, HEAD_DV:]
        o1 = acc_ref[h, tq:, :HEAD_DV] / acc_ref[h, tq:, HEAD_DV:]
        heads.append(_head_out(o0 - lam * o1, subw_ref[:, hs], lam_init))
    ob = jnp.concatenate(heads, axis=1).astype(BF16)
    o_ref[...] = x + gt_ref[...] * _dot(ob, wo_ref[...])


def _attn_prompt_layer(x, tok, mods, l, j, norm_g, wq, ones_blk, qnw, kb, vb, lp, subw, wo,
                       lam_init, out_kind):
    xv, xspec = tok.view(x)
    (m_arr, sh), (_, sc), (_, gt) = tok.mods(mods, l, 3)
    seq, d, tq = tok.t, tok.d, tok.tt
    kvspec = pl.BlockSpec((None, seq, d), lambda b, i: (b, 0, 0), pipeline_mode=pl.Buffered(1))
    return pl.pallas_call(
        functools.partial(_attn_prompt_kernel, lam_init=lam_init),
        grid=tok.grid,
        in_specs=[xspec, _const_spec((1, d)), sh, sc, gt, _layer_spec(wq, j),
                  _const_spec(ones_blk.shape), _const_spec(qnw.shape), kvspec, kvspec,
                  _const_spec(lp.shape), _const_spec(subw.shape), _layer_spec(wo, j)],
        out_specs=tok.out_spec(kind=out_kind),
        out_shape=tok.out_shape(kind=out_kind),
        scratch_shapes=[pltpu.VMEM((N_HEADS, 2 * tq, HEAD_DV), BF16),
                        pltpu.VMEM((N_HEADS, 2 * tq, LANES), F32),
                        pltpu.VMEM((N_HEADS, 2 * tq, 2 * HEAD_DV), F32)],
        compiler_params=_params(2),
        name="diff_attn_prompt",
    )(xv, norm_g, m_arr, m_arr, m_arr, wq, ones_blk, qnw, kb, vb, lp, subw, wo)


def _query_kernel(x_ref, g_ref, sh_ref, sc_ref, wq_ref, ones_ref, qnw_ref, q_ref):
    q = _query(x_ref[...], g_ref[...], sh_ref[...], sc_ref[...], wq_ref[...], ones_ref[...],
               qnw_ref[...])
    q_ref[...] = q.astype(BF16).reshape(q_ref.shape)


def _query_layer(x, tok, mods, l, j, norm_g, wq, ones_blk, qnw):
    xv, xspec = tok.view(x)
    (m_arr, sh), (_, sc) = tok.mods(mods, l, 2)
    return pl.pallas_call(
        _query_kernel,
        grid=tok.grid,
        in_specs=[xspec, _const_spec((1, tok.d)), sh, sc, _layer_spec(wq, j),
                  _const_spec(ones_blk.shape), _const_spec(qnw.shape)],
        out_specs=tok.out_spec(),
        out_shape=tok.out_shape(dtype=BF16),
        compiler_params=_params(len(tok.grid)),
        name="diff_attn_query",
    )(xv, norm_g, m_arr, m_arr, wq, ones_blk, qnw)


def _attn_paged_kernel(pt_ref, q_ref, kn_ref, vn_ref, lp_ref, *rest, n_pages, lam_init):
    k_refs = rest[:n_pages]
    v_refs = rest[n_pages:2 * n_pages]
    o_ref = rest[2 * n_pages]
    qx_ref, m_ref, l_ref, acc_ref = rest[2 * n_pages + 1:]
    c = pl.program_id(1)
    t_new, d = q_ref.shape
    page = k_refs[0].shape[1]

    rows_per_head = 2 * t_new

    @pl.when(c == 0)
    def _():
        qt = jnp.concatenate([q_ref[...].astype(F32)] * (2 * N_HEADS), axis=0)
        row = lax.broadcasted_iota(jnp.int32, qt.shape, 0)
        col = lax.broadcasted_iota(jnp.int32, qt.shape, 1)
        qx_ref[...] = jnp.where(row // t_new == col // HEAD_DK, qt, 0.0).astype(BF16)
        m_ref[...] = jnp.full(m_ref.shape, NEG_BIG, F32)
        l_ref[...] = jnp.zeros(l_ref.shape, F32)
        acc_ref[...] = jnp.zeros(acc_ref.shape, F32)

    def update(s, v_of_head):
        m_old = m_ref[...]
        m_new = jnp.maximum(m_old, jnp.max(s, axis=-1, keepdims=True))
        alpha = jnp.exp(m_old - m_new)
        p = jnp.exp(s - jnp.concatenate([m_new] * (s.shape[1] // LANES), axis=1))
        l_ref[...] = alpha * l_ref[...] + jnp.sum(p, axis=-1, keepdims=True)
        pb = p.astype(BF16)
        for h in range(N_HEADS):
            rs = slice(h * rows_per_head, (h + 1) * rows_per_head)
            acc_ref[rs, :] = alpha[rs] * acc_ref[rs, :] + _dot(pb[rs], v_of_head(h))
        m_ref[...] = m_new

    qx = qx_ref[...]
    kt = jnp.concatenate([r[...].astype(BF16) for r in k_refs], axis=1)

    def cached_v(h):
        return jnp.concatenate(
            [r[pl.ds(h, page, stride=N_HEADS), :].astype(BF16) for r in v_refs], axis=0)

    update(_dot(qx, kt), cached_v)

    @pl.when(c == pl.num_programs(1) - 1)
    def _():
        pad = jnp.zeros((page - t_new, d), F32)
        kb_new = jnp.concatenate([kn_ref[...], pad], axis=0).astype(BF16)
        vb_new = jnp.concatenate([vn_ref[...], pad], axis=0).astype(BF16)
        s = _dot_nt(qx, kb_new)
        row = lax.broadcasted_iota(jnp.int32, s.shape, 0)
        col = lax.broadcasted_iota(jnp.int32, s.shape, 1)
        s = jnp.where(col <= row % t_new, s, NEG_BIG)
        update(s, lambda h: vb_new[:, h * HEAD_DV:(h + 1) * HEAD_DV])
        lam = _diff_lambda(lp_ref, lam_init)
        inv_l = 1.0 / l_ref[...]
        for h in range(N_HEADS):
            r0 = h * rows_per_head
            o0 = acc_ref[r0:r0 + t_new, :] * inv_l[r0:r0 + t_new]
            o1 = acc_ref[r0 + t_new:r0 + 2 * t_new, :] * inv_l[r0 + t_new:r0 + 2 * t_new]
            o_ref[:, h * HEAD_DV:(h + 1) * HEAD_DV] = o0 - lam * o1


def _attn_paged_layer(q_tm, kn_tm, vn_tm, cache_kt, cache_vr, page_table, lp, lam_init, n_pages):
    t_new, nb, d = q_tm.shape
    page = cache_kt.shape[2]
    n_chunks = page_table.shape[1] // n_pages
    n_rows = 2 * N_HEADS * t_new
    row_spec = pl.BlockSpec((t_new, d), lambda b, c, pt: (0, b))

    def page_spec(p, shape):
        return pl.BlockSpec((None,) + shape, lambda b, c, pt: (pt[b, c * n_pages + p], 0, 0))

    k_pages = [page_spec(p, cache_kt.shape[1:]) for p in range(n_pages)]
    v_pages = [page_spec(p, cache_vr.shape[1:]) for p in range(n_pages)]
    grid_spec = pltpu.PrefetchScalarGridSpec(
        num_scalar_prefetch=1,
        grid=(nb, n_chunks),
        in_specs=[row_spec, row_spec, row_spec,
                  pl.BlockSpec(lp.shape, lambda b, c, pt: (0, 0))] + k_pages + v_pages,
        out_specs=row_spec,
        scratch_shapes=[pltpu.VMEM((n_rows, d), BF16), pltpu.VMEM((n_rows, LANES), F32),
                        pltpu.VMEM((n_rows, LANES), F32), pltpu.VMEM((n_rows, HEAD_DV), F32)],
    )
    flat = lambda a: a.reshape(t_new, nb * d)
    o = pl.pallas_call(
        functools.partial(_attn_paged_kernel, n_pages=n_pages, lam_init=lam_init),
        grid_spec=grid_spec,
        out_shape=jax.ShapeDtypeStruct((t_new, nb * d), F32),
        compiler_params=_params(2),
        name="diff_attn_paged",
    )(page_table, flat(q_tm), flat(kn_tm), flat(vn_tm), lp,
      *([cache_kt] * n_pages), *([cache_vr] * n_pages))
    return o.reshape(t_new, nb, d)


def _attn_out_kernel(x_ref, o_ref_in, gt_ref, subw_ref, wo_ref, out_ref, *, lam_init):
    x = x_ref[...]
    d = x.shape[-1]
    o = o_ref_in[...].reshape(-1, d)
    heads = []
    for h in range(N_HEADS):
        hs = slice(h * HEAD_DV, (h + 1) * HEAD_DV)
        heads.append(_head_out(o[:, hs], subw_ref[:, hs], lam_init))
    ob = jnp.concatenate(heads, axis=1).astype(BF16)
    out_ref[...] = x + gt_ref[...] * _dot(ob, wo_ref[...]).reshape(x.shape)


def _attn_out_layer(x, o, tok, mods, l, j, subw, wo, lam_init):
    xv, xspec = tok.view(x)
    ov, ospec = tok.view(o)
    m_arr, gt = tok.mods(mods, l, 3)[2]
    return pl.pallas_call(
        functools.partial(_attn_out_kernel, lam_init=lam_init),
        grid=tok.grid,
        in_specs=[xspec, ospec, gt, _const_spec(subw.shape), _layer_spec(wo, j)],
        out_specs=tok.out_spec(),
        out_shape=tok.out_shape(),
        compiler_params=_params(len(tok.grid)),
        name="diff_attn_out",
    )(xv, ov, m_arr, subw, wo)


def kernel(x_prompt, x_sample, state_ssm_re, state_ssm_im, cache_k, cache_v, page_table, c_prompt, c_sample, norm_mix, w_ada_mix, b_ada_mix, norm_ffn, w_ada_ffn, b_ada_ffn, ffn_w13, ffn_w2, ssm_lam_re, ssm_lam_im, ssm_log_step, ssm_b_re, ssm_b_im, ssm_c_re, ssm_c_im, ssm_d, ssm_w_glu, norm_kv, w_ada_kv, b_ada_kv, w_kv, k_norm, w_q, q_norm, diff_lambda, subln, w_o):
    pb, seq, d = x_prompt.shape
    db, dseq, _ = x_sample.shape
    depth = norm_mix.shape[0]
    n_a, g, n = ssm_lam_re.shape
    gn = g * n
    nk = N_HEADS * 2 * HEAD_DK

    c_all = jnp.concatenate([c_sample, c_prompt], axis=0)
    mods_mix = _Mods(_ada_mods(c_all, w_ada_mix, b_ada_mix), d, db)
    mods_ffn = _Mods(_ada_mods(c_all, w_ada_ffn, b_ada_ffn), d, db)
    mods_kv = _Mods(_ada_mods(c_all, w_ada_kv[None], b_ada_kv[None]), d, db)

    a_re, a_im, bb_re, bb_im = _discretize(ssm_lam_re, ssm_lam_im, ssm_log_step, ssm_b_re, ssm_b_im)
    wb = _block_diag_in(bb_re, bb_im, g, n)
    wc = _block_diag_out(ssm_c_re, ssm_c_im)
    w13, w2, wglu = ffn_w13.astype(BF16), ffn_w2.astype(BF16), ssm_w_glu.astype(BF16)
    wkv, wq, wo = w_kv.astype(BF16), w_q.astype(BF16), w_o.astype(BF16)
    row = lambda a: a.reshape(1, -1)
    knw = row(jnp.tile(k_norm.reshape(-1), N_HEADS))
    qnw = [row(jnp.tile(q_norm[j].reshape(-1), N_HEADS)) for j in range(depth - n_a)]
    subw = [row(jnp.tile(subln[j], N_HEADS)) for j in range(depth - n_a)]
    blk = jnp.arange(MXU_DIM) // HEAD_DK
    ones_blk = ((blk[:, None] == blk[None, :]).astype(F32) / HEAD_DK).astype(BF16)
    lam_inits = [0.8 - 0.6 * math.exp(-0.3 * layer) for layer in range(n_a, depth)]

    n_pool, page = cache_k.shape[:2]
    ck = cache_k.transpose(0, 2, 3, 4, 1).reshape(n_pool, nk, page)
    cv = cache_v.reshape(n_pool, page * N_HEADS, HEAD_DV)

    def run_group(x_tm, h0_re, h0_im, row_block, tc, toks, paged):
        tok_pre, tok_at0, tok_at, tok_post = toks
        nb = x_tm.shape[1]
        new_re, new_im = [], []
        x = x_tm
        for a in range(n_a):
            x, hr, hi = _s5_layer(x.reshape(x_tm.shape), mods_mix, a, row_block, row(norm_mix[a]),
                                  h0_re[a], h0_im[a], a_re[a], a_im[a], wb, wc, row(ssm_d[a]),
                                  wglu, tc)
            new_re.append(hr.reshape(nb, g, n))
            new_im.append(hi.reshape(nb, g, n))
            x = _ffn_layer(x, tok_pre, mods_ffn, a, row(norm_ffn[a]), w13, w2)
        k, v, kb, vb = _kv_layer(x, tok_pre, mods_kv, row(norm_kv), wkv, ones_blk, knw,
                                 out_kind=tok_post.kind)
        for j in range(depth - n_a):
            layer = n_a + j
            tok = tok_at0 if j == 0 else tok_at
            if paged:
                q = _query_layer(x, tok, mods_mix, layer, j, row(norm_mix[layer]), wq, ones_blk, qnw[j])
                o = _attn_paged_layer(q, k, v, ck, cv, page_table, diff_lambda[j], lam_inits[j], 8)
                x = _attn_out_layer(x, o, tok, mods_mix, layer, j, subw[j], wo, lam_inits[j])
            else:
                x = _attn_prompt_layer(x, tok, mods_mix, layer, j, row(norm_mix[layer]), wq, ones_blk,
                                       qnw[j], kb, vb, diff_lambda[j], subw[j], wo, lam_inits[j],
                                       out_kind=tok_post.kind)
            x = _ffn_layer(x, tok_post, mods_ffn, layer, row(norm_ffn[layer]), w13, w2)
        return x, jnp.stack(new_re), jnp.stack(new_im), k, v

    zeros = jnp.zeros((n_a, pb, gn), F32)
    toks_p = (_Tokens("tm", pb, seq, d, tt=512), _Tokens("tm", pb, seq, d, tt=512),
              _Tokens("bm", pb, seq, d, tt=512), _Tokens("bm", pb, seq, d, tt=512))
    y_p, re_p, im_p, k_p, v_p = run_group(
        x_prompt.transpose(1, 0, 2), zeros, zeros, db // pb, 64, toks_p, False)

    tok_s = _Tokens("slab", db, dseq, d)
    y_s, re_s, im_s, k_s, v_s = run_group(
        x_sample.transpose(1, 0, 2), state_ssm_re.reshape(n_a, db, gn).astype(F32),
        state_ssm_im.reshape(n_a, db, gn).astype(F32), 0, dseq, (tok_s,) * 4, True)
    tm2bm = lambda a: a.transpose(1, 0, 2)

    return (y_p, tm2bm(y_s), re_p, im_p, re_s, im_s,
            k_p.reshape(pb, seq, N_HEADS, 2, HEAD_DK), v_p.reshape(pb, seq, N_HEADS, HEAD_DV),
            tm2bm(k_s).reshape(db, dseq, N_HEADS, 2, HEAD_DK), tm2bm(v_s).reshape(db, dseq, N_HEADS, HEAD_DV))
```

```python
import functools
import math

import jax
import jax.numpy as jnp
from jax import lax
from jax.experimental import pallas as pl
from jax.experimental.pallas import tpu as pltpu

F32 = jnp.float32
BF16 = jnp.bfloat16

EPS = 1e-6
N_HEADS = 8
HEAD_DK = 64
HEAD_DV = 2 * HEAD_DK
ATTN_SCALE = HEAD_DK ** -0.5
SSM_GROUP = 16
SSM_STATE = 64
LANES = 128
MXU_DIM = 256
GROUPS_PER_BLOCK = MXU_DIM // SSM_GROUP
VMEM_LIMIT = 56 * 1024 * 1024
NEG_BIG = -1e30
PROMPT_ROWS = 512
LOG2_E = math.log2(math.e)


def _params(n_grid, vmem=VMEM_LIMIT):
    return pltpu.CompilerParams(dimension_semantics=("arbitrary",) * n_grid,
                                vmem_limit_bytes=vmem)


def _const_spec(shape):
    nd = len(shape)
    return pl.BlockSpec(tuple(shape), lambda *_: (0,) * nd, pipeline_mode=pl.Buffered(1))


def _layer_spec(stacked, l):
    nd = stacked.ndim - 1
    return pl.BlockSpec((None,) + stacked.shape[1:], lambda *_: (l,) + (0,) * nd,
                        pipeline_mode=pl.Buffered(1))


def _dot(a, b):
    return jnp.dot(a, b, preferred_element_type=F32)


def _dot_nt(a, b):
    return lax.dot_general(a, b, (((1,), (1,)), ((), ())), preferred_element_type=F32)


def _modulate(x, g, shift, scale):
    ms = jnp.mean(x * x, axis=-1, keepdims=True)
    xn = x * lax.rsqrt(ms + EPS) * g
    return xn * (1.0 + scale) + shift


def _group_rms(y, ones_blk, w):
    parts = []
    for j in range(y.shape[1] // MXU_DIM):
        blk = y[:, j * MXU_DIM:(j + 1) * MXU_DIM]
        parts.append(_dot((blk * blk).astype(BF16), ones_blk))
    ms = jnp.concatenate(parts, axis=1)
    return y * lax.rsqrt(ms + EPS) * w


CAST_BLOCK_BYTES = 6 * 1024 * 1024


def _cast_kernel(w_ref, o_ref):
    o_ref[...] = w_ref[...].astype(o_ref.dtype)


def _to_bf16(w):
    cols = w.shape[-1]
    w2 = w.reshape(-1, cols)
    n_rows = w2.shape[0]
    rb = n_rows
    while rb * cols * 4 > CAST_BLOCK_BYTES and rb % 32 == 0:
        rb //= 2
    out = pl.pallas_call(
        _cast_kernel,
        grid=(n_rows // rb,),
        in_specs=[pl.BlockSpec((rb, cols), lambda i: (i, 0))],
        out_specs=pl.BlockSpec((rb, cols), lambda i: (i, 0)),
        out_shape=jax.ShapeDtypeStruct(w2.shape, BF16),
        compiler_params=_params(1),
        name="cast_bf16",
    )(w2)
    return out.reshape(w.shape)


def _ada_kernel(c_ref, w_ref, b_ref, o_ref):
    c = c_ref[...]
    s = (c * jax.nn.sigmoid(c)).astype(BF16)
    o_ref[...] = _dot(s, w_ref[...].astype(BF16)) + b_ref[...]


def _ada_mods(c_all, w, b):
    nl, d, n = w.shape
    r = c_all.shape[0]
    tn = 1024
    return pl.pallas_call(
        _ada_kernel,
        grid=(nl, n // tn),
        in_specs=[pl.BlockSpec((r, d), lambda l, j: (0, 0)),
                  pl.BlockSpec((None, d, tn), lambda l, j: (l, 0, j)),
                  pl.BlockSpec((None, 1, tn), lambda l, j: (l, 0, j))],
        out_specs=pl.BlockSpec((None, r, tn), lambda l, j: (l, 0, j)),
        out_shape=jax.ShapeDtypeStruct((nl, r, n), F32),
        compiler_params=_params(2),
        name="ada_mods",
    )(c_all, w, b.reshape(nl, 1, n))


class _Mods:
    def __init__(self, arr, d, db):
        self.a3 = arr
        self.a4 = arr.reshape(arr.shape[0], arr.shape[1], 1, arr.shape[2])
        self.d = d
        self.db = db

    def spec_rows(self, l, k, row_block, n_rows):
        return self.a3, pl.BlockSpec((None, n_rows, self.d), lambda *_: (l, row_block, k))

    def spec_prompt_row(self, l, k):
        db = self.db
        return self.a4, pl.BlockSpec((None, None, 1, self.d), lambda b, *_: (l, db + b, 0, k))


def _disc_kernel(lr_ref, li_ref, ls_ref, br_ref, bi_ref, ar_ref, ai_ref, wb_ref, *, n_state):
    dt = jnp.exp(ls_ref[...])
    lr, li = lr_ref[...], li_ref[...]
    mag = jnp.exp(lr * dt)
    a_re, a_im = mag * jnp.cos(li * dt), mag * jnp.sin(li * dt)
    nr, ni = a_re - 1.0, a_im
    den = lr * lr + li * li
    f_re = (nr * lr + ni * li) / den
    f_im = (ni * lr - nr * li) / den
    br, bi = br_ref[...], bi_ref[...]
    ar_ref[...] = a_re
    ai_ref[...] = a_im
    bb_re = f_re * br - f_im * bi
    bb_im = f_re * bi + f_im * br
    nj, kin, two_half = wb_ref.shape
    half = two_half // 2
    p = br.shape[0]
    row = lax.broadcasted_iota(jnp.int32, (kin, half), 0)
    col = lax.broadcasted_iota(jnp.int32, (kin, half), 1)
    own = row // p == col // n_state
    for j in range(nj):
        cols = slice(j * half, (j + 1) * half)
        for part, bb in enumerate((bb_re, bb_im)):
            tiled = jnp.concatenate([bb[:, cols]] * (kin // p), axis=0)
            wb_ref[j, :, part * half:(part + 1) * half] = jnp.where(own, tiled, 0.0).astype(BF16)


def _discretize(lam_re, lam_im, log_step, b_re, b_im):
    na, g, n = lam_re.shape
    p = b_re.shape[-1]
    gn = g * n
    gb = GROUPS_PER_BLOCK
    nj = g // gb
    vec = lambda a: a.reshape(na, 1, gn)
    ls = jnp.broadcast_to(log_step[:, :, None], (na, g, n)).reshape(na, 1, gn)
    bt = lambda a: a.transpose(0, 3, 1, 2).reshape(na, p, gn)
    row = pl.BlockSpec((None, 1, gn), lambda a: (a, 0, 0))
    mat = pl.BlockSpec((None, p, gn), lambda a: (a, 0, 0))
    wb_shape = (na, nj, gb * p, 2 * gb * n)
    return pl.pallas_call(
        functools.partial(_disc_kernel, n_state=n),
        grid=(na,),
        in_specs=[row, row, row, mat, mat],
        out_specs=[row, row, pl.BlockSpec((None,) + wb_shape[1:], lambda a: (a, 0, 0, 0))],
        out_shape=[jax.ShapeDtypeStruct((na, 1, gn), F32)] * 2
        + [jax.ShapeDtypeStruct(wb_shape, BF16)],
        compiler_params=_params(1),
        name="s5_discretize",
    )(vec(lam_re), vec(lam_im), ls, bt(b_re), bt(b_im))


def _block_diag_out(c_re, c_im):
    na, g, p, n = c_re.shape
    gb = GROUPS_PER_BLOCK
    nj = g // gb
    eye = jnp.eye(gb, dtype=F32)

    def one(c):
        c5 = c.reshape(na, nj, gb, p, n)
        w = jnp.einsum('ajgpn,gh->ajgnhp', c5, eye)
        return w.reshape(na, nj, gb * n, gb * p)

    return jnp.concatenate([one(c_re), one(-c_im)], axis=2).astype(BF16)


def _s5_kernel(x_ref, g_ref, sh_ref, sc_ref, gt_ref, h0r_ref, h0i_ref, ar_ref, ai_ref,
               wb_ref, wc_ref, dsk_ref, wglu_ref,
               xo_ref, hfr_ref, hfi_ref,
               str_ref, sti_ref, bu_ref, y_ref, xt_ref, *, col_w, bm_in, bm_out):
    if bm_in:
        nb, tc, d = x_ref.shape
    else:
        tc, nb, d = x_ref.shape
    rows = tc * nb
    nj = wb_ref.shape[0]
    half = wb_ref.shape[2] // 2

    @pl.when(pl.program_id(0) == 0)
    def _():
        str_ref[...] = h0r_ref[...]
        sti_ref[...] = h0i_ref[...]

    lane_blocks = [slice(c * LANES, (c + 1) * LANES) for c in range(d // LANES)]
    if bm_in:
        for b in range(nb):
            for c, cols in enumerate(lane_blocks):
                xt_ref[c, pl.ds(b, tc, stride=nb), :] = x_ref[b, :, cols]
        x = jnp.concatenate([xt_ref[c] for c in range(len(lane_blocks))], axis=1)
        x = x.reshape(tc, nb, d)
    else:
        x = x_ref[...]
    u = _modulate(x, g_ref[...], sh_ref[...], sc_ref[...]).reshape(rows, d)
    ub = u.astype(BF16)

    for j in range(nj):
        kin = wb_ref.shape[1]
        bu_ref[...] = _dot(ub[:, j * kin:(j + 1) * kin], wb_ref[j])
        for cs in range(0, half, col_w):
            g0 = j * half + cs
            ar = jnp.broadcast_to(ar_ref[:, g0:g0 + col_w], (nb, col_w))
            ai = jnp.broadcast_to(ai_ref[:, g0:g0 + col_w], (nb, col_w))

            def step(t, carry, cs=cs, ar=ar, ai=ai):
                hr, hi = carry
                r0 = pl.multiple_of(t * nb, nb)
                bur = bu_ref[pl.ds(r0, nb), cs:cs + col_w]
                bui = bu_ref[pl.ds(r0, nb), half + cs:half + cs + col_w]
                nhr = ar * hr - ai * hi + bur
                nhi = ar * hi + ai * hr + bui
                bu_ref[pl.ds(r0, nb), cs:cs + col_w] = nhr
                bu_ref[pl.ds(r0, nb), half + cs:half + cs + col_w] = nhi
                return nhr, nhi

            hr, hi = lax.fori_loop(0, tc, step,
                                   (str_ref[:, g0:g0 + col_w], sti_ref[:, g0:g0 + col_w]),
                                   unroll=True)
            str_ref[:, g0:g0 + col_w] = hr
            sti_ref[:, g0:g0 + col_w] = hi
        kout = wc_ref.shape[2]
        y_ref[:, j * kout:(j + 1) * kout] = _dot(bu_ref[...].astype(BF16), wc_ref[j])

    y = y_ref[...] + dsk_ref[...] * u
    z = _dot(jax.nn.gelu(y).astype(BF16), wglu_ref[...])
    out = z[:, :d] * jax.nn.sigmoid(z[:, d:])
    res = x + gt_ref[...] * out.reshape(tc, nb, d)
    if bm_out:
        res = res.reshape(rows, d)
        for c, cols in enumerate(lane_blocks):
            xt_ref[c] = res[:, cols]
        for b in range(nb):
            for c, cols in enumerate(lane_blocks):
                xo_ref[b, :, cols] = xt_ref[c, pl.ds(b, tc, stride=nb), :]
    else:
        xo_ref[...] = res
    hfr_ref[...] = str_ref[...]
    hfi_ref[...] = sti_ref[...]


def _s5_layer(x, tc, bm_in, bm_out, mods, l, row_block, norm_g, h0_re, h0_im, a_re, a_im,
              wb, wc, dskip, wglu):
    if bm_in:
        nb, t, d = x.shape
    else:
        t, nb, d = x.shape
    gn = a_re.shape[-1]
    rows = tc * nb
    col_w = max(LANES, (8 * 512) // nb)
    tm_spec = pl.BlockSpec((tc, nb, d), lambda c: (c, 0, 0))
    bm_spec = pl.BlockSpec((nb, tc, d), lambda c: (0, c, 0))
    (m_arr, sh), (_, sc), (_, gt) = (mods.spec_rows(l, k, row_block, nb) for k in range(3))
    st_spec = pl.BlockSpec((nb, gn), lambda c: (0, 0))
    return pl.pallas_call(
        functools.partial(_s5_kernel, col_w=col_w, bm_in=bm_in, bm_out=bm_out),
        grid=(t // tc,),
        in_specs=[bm_spec if bm_in else tm_spec, _const_spec((1, d)), sh, sc, gt, st_spec, st_spec,
                  _const_spec((1, gn)), _const_spec((1, gn)),
                  _layer_spec(wb, l), _layer_spec(wc, l),
                  _const_spec((1, d)), _layer_spec(wglu, l)],
        out_specs=[bm_spec if bm_out else tm_spec, st_spec, st_spec],
        out_shape=[jax.ShapeDtypeStruct((nb, t, d) if bm_out else (t, nb, d), F32),
                   jax.ShapeDtypeStruct((nb, gn), F32), jax.ShapeDtypeStruct((nb, gn), F32)],
        scratch_shapes=[pltpu.VMEM((nb, gn), F32), pltpu.VMEM((nb, gn), F32),
                        pltpu.VMEM((rows, wb.shape[3]), F32), pltpu.VMEM((rows, d), F32),
                        pltpu.VMEM((d // LANES, rows, LANES) if (bm_in or bm_out)
                                   else (1, 8, LANES), F32)],
        compiler_params=_params(1),
        name="s5_layer",
    )(x, norm_g, m_arr, m_arr, m_arr, h0_re, h0_im, a_re, a_im, wb, wc, dskip, wglu)


class _Tokens:
    def __init__(self, kind, nb, t, d, rows, mod_row_block=0):
        self.kind, self.nb, self.t, self.d = kind, nb, t, d
        self.tt = rows // nb if kind == "tm" else rows
        self.mod_row_block = mod_row_block

    @property
    def grid(self):
        return (self.t // self.tt,) if self.kind == "tm" else (self.nb, self.t // self.tt)

    def spec(self, width=None):
        w = self.d if width is None else width
        if self.kind == "tm":
            return pl.BlockSpec((self.tt, self.nb, w), lambda i: (i, 0, 0))
        return pl.BlockSpec((None, self.tt, w), lambda b, i: (b, i, 0))

    def shape(self, dtype=F32, width=None):
        w = self.d if width is None else width
        dims = (self.t, self.nb, w) if self.kind == "tm" else (self.nb, self.t, w)
        return jax.ShapeDtypeStruct(dims, dtype)

    def mods(self, mods, l, n):
        if self.kind == "tm":
            return [mods.spec_rows(l, k, self.mod_row_block, self.nb) for k in range(n)]
        return [mods.spec_prompt_row(l, k) for k in range(n)]


def _ffn_chunks(f):
    out, c0 = [], 0
    while c0 < f:
        c1 = min(f, c0 + 1024)
        out.append((c0, c1))
        c0 = c1
    return out


def _ffn_kernel(x_ref, g_ref, sh_ref, sc_ref, gt_ref, w13_ref, w2_ref, o_ref):
    x = x_ref[...]
    d = x.shape[-1]
    f = w2_ref.shape[0]
    h = _modulate(x, g_ref[...], sh_ref[...], sc_ref[...]).reshape(-1, d).astype(BF16)
    acc = None
    for c0, c1 in _ffn_chunks(f):
        a = _dot(h, w13_ref[:, c0:c1])
        b = _dot(h, w13_ref[:, f + c0:f + c1])
        gte = (a * jax.nn.sigmoid(a) * b).astype(BF16)
        part = _dot(gte, w2_ref[c0:c1, :])
        acc = part if acc is None else acc + part
    o_ref[...] = x + gt_ref[...] * acc.reshape(x.shape)


def _ffn_layer(x, tok, mods, l, norm_g, w13, w2):
    (m_arr, sh), (_, sc), (_, gt) = tok.mods(mods, l, 3)
    return pl.pallas_call(
        _ffn_kernel,
        grid=tok.grid,
        in_specs=[tok.spec(), _const_spec((1, tok.d)), sh, sc, gt,
                  _layer_spec(w13, l), _layer_spec(w2, l)],
        out_specs=tok.spec(),
        out_shape=tok.shape(),
        compiler_params=_params(len(tok.grid)),
        name="swiglu_ffn",
    )(x, norm_g, m_arr, m_arr, m_arr, w13, w2)


def _kv_kernel(x_ref, g_ref, sh_ref, sc_ref, wkv_ref, ones_ref, knw_ref,
               k_ref, v_ref, kb_ref, vb_ref):
    x = x_ref[...]
    d = x.shape[-1]
    h = _modulate(x, g_ref[...], sh_ref[...], sc_ref[...]).reshape(-1, d).astype(BF16)
    kv = _dot(h, wkv_ref[...])
    nk = k_ref.shape[-1]
    k = _group_rms(kv[:, :nk], ones_ref[...], knw_ref[...])
    v = kv[:, nk:]
    k_ref[...] = k.reshape(k_ref.shape)
    v_ref[...] = v.reshape(v_ref.shape)
    kb_ref[...] = k.astype(BF16).reshape(kb_ref.shape)
    vb_ref[...] = v.astype(BF16).reshape(vb_ref.shape)


def _kv_layer(x, tok, mods, norm_g, wkv, ones_blk, knw):
    (m_arr, sh), (_, sc) = tok.mods(mods, 0, 2)
    nk = knw.shape[-1]
    nv = wkv.shape[1] - nk
    widths = (nk, nv, nk, nv)
    dtypes = (F32, F32, BF16, BF16)
    return pl.pallas_call(
        _kv_kernel,
        grid=tok.grid,
        in_specs=[tok.spec(), _const_spec((1, tok.d)), sh, sc, _const_spec(wkv.shape),
                  _const_spec(ones_blk.shape), _const_spec(knw.shape)],
        out_specs=[tok.spec(width=w) for w in widths],
        out_shape=[tok.shape(dtype=t, width=w) for w, t in zip(widths, dtypes)],
        compiler_params=_params(len(tok.grid)),
        name="shared_kv",
    )(x, norm_g, m_arr, m_arr, wkv, ones_blk, knw)


def _diff_lambda(lp_ref, lam_init):
    lp = lp_ref[...]
    return (jnp.exp(jnp.sum(lp[0:1] * lp[1:2], axis=-1, keepdims=True))
            - jnp.exp(jnp.sum(lp[2:3] * lp[3:4], axis=-1, keepdims=True)) + lam_init)


def _query(x, g, sh, sc, wq, ones_blk, qnw):
    h = _modulate(x, g, sh, sc).reshape(-1, x.shape[-1]).astype(BF16)
    q = _group_rms(_dot(h, wq), ones_blk, qnw)
    return q * ATTN_SCALE


def _head_out(o, subw, lam_init):
    ms = jnp.mean(o * o, axis=-1, keepdims=True)
    return o * lax.rsqrt(ms + EPS) * subw * (1.0 - lam_init)


def _attn_prompt_kernel(x_ref, g_ref, sh_ref, sc_ref, gt_ref, wq_ref, ones_ref, qnw_ref,
                        kb_ref, vb_ref, lp_ref, subw_ref, wo_ref,
                        o_ref,
                        qs_ref, m_ref, acc_ref, *, lam_init):
    tq, d = x_ref.shape
    tk = tq
    i = pl.program_id(1)
    x = x_ref[...]
    q = _query(x, g_ref[...], sh_ref[...], sc_ref[...], wq_ref[...], ones_ref[...], qnw_ref[...])
    q = q * LOG2_E
    lane = lax.broadcasted_iota(jnp.int32, (tq, HEAD_DV), 1)
    first_map = lane < HEAD_DK
    th = tq // 2
    for h in range(N_HEADS):
        qh = q[:, h * HEAD_DV:(h + 1) * HEAD_DV]
        q0 = jnp.where(first_map, qh, 0.0).astype(BF16)
        q1 = jnp.where(first_map, 0.0, qh).astype(BF16)
        qs_ref[h] = jnp.concatenate([q0[:th], q1[:th], q0[th:], q1[th:]], axis=0)
    m_ref[...] = jnp.full(m_ref.shape, NEG_BIG, F32)
    acc_ref[...] = jnp.zeros(acc_ref.shape, F32)

    def attend(k0, n_keys, r0, n_rows, first_pos):
        for h in range(N_HEADS):
            hs = slice(h * HEAD_DV, (h + 1) * HEAD_DV)
            kh = kb_ref[pl.ds(k0, n_keys), hs]
            vh = vb_ref[pl.ds(k0, n_keys), hs]
            v_ext = jnp.concatenate([vh, jnp.ones_like(vh)], axis=1)
            s = _dot_nt(qs_ref[h, r0:r0 + n_rows, :], kh)
            if first_pos is not None:
                row = lax.broadcasted_iota(jnp.int32, s.shape, 0)
                col = lax.broadcasted_iota(jnp.int32, s.shape, 1)
                s = jnp.where(col <= row % th + first_pos, s, NEG_BIG)
            m_old = m_ref[h, r0:r0 + n_rows, :]
            m_new = jnp.maximum(m_old, jnp.max(s, axis=-1, keepdims=True))
            alpha = jnp.exp2(m_old - m_new)
            p = jnp.exp2(s - jnp.concatenate([m_new] * (n_keys // LANES), axis=1))
            acc_ref[h, r0:r0 + n_rows, :] = (
                jnp.concatenate([alpha, alpha], axis=1) * acc_ref[h, r0:r0 + n_rows, :]
                + _dot(p.astype(BF16), v_ext))
            m_ref[h, r0:r0 + n_rows, :] = m_new

    def body(j, carry):
        attend(pl.multiple_of(j * tk, tk), tk, 0, 2 * tq, None)
        return carry

    lax.fori_loop(0, i, body, 0)
    kd = pl.multiple_of(i * tk, tk)
    attend(kd, th, 0, tq, 0)
    attend(kd, tk, tq, tq, th)

    lam = _diff_lambda(lp_ref, lam_init)
    heads = []
    for h in range(N_HEADS):
        hs = slice(h * HEAD_DV, (h + 1) * HEAD_DV)
        o = acc_ref[h, :, :HEAD_DV] / acc_ref[h, :, HEAD_DV:]
        o0 = jnp.concatenate([o[:th], o[2 * th:3 * th]], axis=0)
        o1 = jnp.concatenate([o[th:2 * th], o[3 * th:]], axis=0)
        heads.append(_head_out(o0 - lam * o1, subw_ref[:, hs], lam_init))
    ob = jnp.concatenate(heads, axis=1).astype(BF16)
    o_ref[...] = x + gt_ref[...] * _dot(ob, wo_ref[...])


def _attn_prompt_layer(x, tok, mods, l, j, norm_g, wq, ones_blk, qnw, kb, vb, lp, subw, wo,
                       lam_init):
    (m_arr, sh), (_, sc), (_, gt) = tok.mods(mods, l, 3)
    seq, d, tq = tok.t, tok.d, tok.tt
    kvspec = pl.BlockSpec((None, seq, d), lambda b, i: (b, 0, 0), pipeline_mode=pl.Buffered(1))
    return pl.pallas_call(
        functools.partial(_attn_prompt_kernel, lam_init=lam_init),
        grid=tok.grid,
        in_specs=[tok.spec(), _const_spec((1, d)), sh, sc, gt, _layer_spec(wq, j),
                  _const_spec(ones_blk.shape), _const_spec(qnw.shape), kvspec, kvspec,
                  _const_spec(lp.shape), _const_spec(subw.shape), _layer_spec(wo, j)],
        out_specs=tok.spec(),
        out_shape=tok.shape(),
        scratch_shapes=[pltpu.VMEM((N_HEADS, 2 * tq, HEAD_DV), BF16),
                        pltpu.VMEM((N_HEADS, 2 * tq, LANES), F32),
                        pltpu.VMEM((N_HEADS, 2 * tq, 2 * HEAD_DV), F32)],
        compiler_params=_params(2),
        name="diff_attn_prompt",
    )(x, norm_g, m_arr, m_arr, m_arr, wq, ones_blk, qnw, kb, vb, lp, subw, wo)


def _query_kernel(x_ref, g_ref, sh_ref, sc_ref, wq_ref, ones_ref, qnw_ref, q_ref):
    q = _query(x_ref[...], g_ref[...], sh_ref[...], sc_ref[...], wq_ref[...], ones_ref[...],
               qnw_ref[...])
    q_ref[...] = q.astype(BF16).reshape(q_ref.shape)


def _query_layer(x, tok, mods, l, j, norm_g, wq, ones_blk, qnw):
    (m_arr, sh), (_, sc) = tok.mods(mods, l, 2)
    return pl.pallas_call(
        _query_kernel,
        grid=tok.grid,
        in_specs=[tok.spec(), _const_spec((1, tok.d)), sh, sc, _layer_spec(wq, j),
                  _const_spec(ones_blk.shape), _const_spec(qnw.shape)],
        out_specs=tok.spec(),
        out_shape=tok.shape(dtype=BF16),
        compiler_params=_params(len(tok.grid)),
        name="diff_attn_query",
    )(x, norm_g, m_arr, m_arr, wq, ones_blk, qnw)


def _attn_paged_kernel(pt_ref, q_ref, kn_ref, vn_ref, lp_ref, *rest, n_pages, lam_init):
    k_refs = rest[:n_pages]
    v_refs = rest[n_pages:2 * n_pages]
    o_ref = rest[2 * n_pages]
    qx_ref, m_ref, l_ref, acc_ref = rest[2 * n_pages + 1:]
    c = pl.program_id(1)
    t_new, d = q_ref.shape
    page = k_refs[0].shape[1]

    rows_per_head = 2 * t_new

    @pl.when(c == 0)
    def _():
        qt = jnp.concatenate([q_ref[...].astype(F32)] * (2 * N_HEADS), axis=0)
        row = lax.broadcasted_iota(jnp.int32, qt.shape, 0)
        col = lax.broadcasted_iota(jnp.int32, qt.shape, 1)
        qx_ref[...] = jnp.where(row // t_new == col // HEAD_DK, qt, 0.0).astype(BF16)
        m_ref[...] = jnp.full(m_ref.shape, NEG_BIG, F32)
        l_ref[...] = jnp.zeros(l_ref.shape, F32)
        acc_ref[...] = jnp.zeros(acc_ref.shape, F32)

    def update(s, v_of_head):
        m_old = m_ref[...]
        m_new = jnp.maximum(m_old, jnp.max(s, axis=-1, keepdims=True))
        alpha = jnp.exp(m_old - m_new)
        p = jnp.exp(s - jnp.concatenate([m_new] * (s.shape[1] // LANES), axis=1))
        l_ref[...] = alpha * l_ref[...] + jnp.sum(p, axis=-1, keepdims=True)
        pb = p.astype(BF16)
        for h in range(N_HEADS):
            rs = slice(h * rows_per_head, (h + 1) * rows_per_head)
            acc_ref[rs, :] = alpha[rs] * acc_ref[rs, :] + _dot(pb[rs], v_of_head(h))
        m_ref[...] = m_new

    qx = qx_ref[...]
    kt = jnp.concatenate([r[...].astype(BF16) for r in k_refs], axis=1)

    def cached_v(h):
        return jnp.concatenate(
            [r[pl.ds(h, page, stride=N_HEADS), :].astype(BF16) for r in v_refs], axis=0)

    update(_dot(qx, kt), cached_v)

    @pl.when(c == pl.num_programs(1) - 1)
    def _():
        pad = jnp.zeros((page - t_new, d), F32)
        kb_new = jnp.concatenate([kn_ref[...], pad], axis=0).astype(BF16)
        vb_new = jnp.concatenate([vn_ref[...], pad], axis=0).astype(BF16)
        s = _dot_nt(qx, kb_new)
        row = lax.broadcasted_iota(jnp.int32, s.shape, 0)
        col = lax.broadcasted_iota(jnp.int32, s.shape, 1)
        s = jnp.where(col <= row % t_new, s, NEG_BIG)
        update(s, lambda h: vb_new[:, h * HEAD_DV:(h + 1) * HEAD_DV])
        lam = _diff_lambda(lp_ref, lam_init)
        inv_l = 1.0 / l_ref[...]
        for h in range(N_HEADS):
            r0 = h * rows_per_head
            o0 = acc_ref[r0:r0 + t_new, :] * inv_l[r0:r0 + t_new]
            o1 = acc_ref[r0 + t_new:r0 + 2 * t_new, :] * inv_l[r0 + t_new:r0 + 2 * t_new]
            o_ref[:, h * HEAD_DV:(h + 1) * HEAD_DV] = o0 - lam * o1


def _attn_paged_layer(q_tm, kn_tm, vn_tm, cache_kt, cache_vr, page_table, lp, lam_init, n_pages):
    t_new, nb, d = q_tm.shape
    page = cache_kt.shape[2]
    n_chunks = page_table.shape[1] // n_pages
    n_rows = 2 * N_HEADS * t_new
    row_spec = pl.BlockSpec((t_new, d), lambda b, c, pt: (0, b))

    def page_spec(p, shape):
        return pl.BlockSpec((None,) + shape, lambda b, c, pt: (pt[b, c * n_pages + p], 0, 0))

    k_pages = [page_spec(p, cache_kt.shape[1:]) for p in range(n_pages)]
    v_pages = [page_spec(p, cache_vr.shape[1:]) for p in range(n_pages)]
    grid_spec = pltpu.PrefetchScalarGridSpec(
        num_scalar_prefetch=1,
        grid=(nb, n_chunks),
        in_specs=[row_spec, row_spec, row_spec,
                  pl.BlockSpec(lp.shape, lambda b, c, pt: (0, 0))] + k_pages + v_pages,
        out_specs=row_spec,
        scratch_shapes=[pltpu.VMEM((n_rows, d), BF16), pltpu.VMEM((n_rows, LANES), F32),
                        pltpu.VMEM((n_rows, LANES), F32), pltpu.VMEM((n_rows, HEAD_DV), F32)],
    )
    flat = lambda a: a.reshape(t_new, nb * d)
    o = pl.pallas_call(
        functools.partial(_attn_paged_kernel, n_pages=n_pages, lam_init=lam_init),
        grid_spec=grid_spec,
        out_shape=jax.ShapeDtypeStruct((t_new, nb * d), F32),
        compiler_params=_params(2),
        name="diff_attn_paged",
    )(page_table, flat(q_tm), flat(kn_tm), flat(vn_tm), lp,
      *([cache_kt] * n_pages), *([cache_vr] * n_pages))
    return o.reshape(t_new, nb, d)


def _attn_out_kernel(x_ref, o_ref_in, gt_ref, subw_ref, wo_ref, out_ref, *, lam_init):
    x = x_ref[...]
    d = x.shape[-1]
    o = o_ref_in[...].reshape(-1, d)
    heads = []
    for h in range(N_HEADS):
        hs = slice(h * HEAD_DV, (h + 1) * HEAD_DV)
        heads.append(_head_out(o[:, hs], subw_ref[:, hs], lam_init))
    ob = jnp.concatenate(heads, axis=1).astype(BF16)
    out_ref[...] = x + gt_ref[...] * _dot(ob, wo_ref[...]).reshape(x.shape)


def _attn_out_layer(x, o, tok, mods, l, j, subw, wo, lam_init):
    m_arr, gt = tok.mods(mods, l, 3)[2]
    return pl.pallas_call(
        functools.partial(_attn_out_kernel, lam_init=lam_init),
        grid=tok.grid,
        in_specs=[tok.spec(), tok.spec(), gt, _const_spec(subw.shape), _layer_spec(wo, j)],
        out_specs=tok.spec(),
        out_shape=tok.shape(),
        compiler_params=_params(len(tok.grid)),
        name="diff_attn_out",
    )(x, o, m_arr, subw, wo)


def kernel(x_prompt, x_sample, state_ssm_re, state_ssm_im, cache_k, cache_v, page_table, c_prompt, c_sample, norm_mix, w_ada_mix, b_ada_mix, norm_ffn, w_ada_ffn, b_ada_ffn, ffn_w13, ffn_w2, ssm_lam_re, ssm_lam_im, ssm_log_step, ssm_b_re, ssm_b_im, ssm_c_re, ssm_c_im, ssm_d, ssm_w_glu, norm_kv, w_ada_kv, b_ada_kv, w_kv, k_norm, w_q, q_norm, diff_lambda, subln, w_o):
    pb, seq, d = x_prompt.shape
    db, dseq, _ = x_sample.shape
    depth = norm_mix.shape[0]
    n_a, g, n = ssm_lam_re.shape
    gn = g * n
    nk = N_HEADS * 2 * HEAD_DK

    c_all = jnp.concatenate([c_sample, c_prompt], axis=0)
    mods_mix = _Mods(_ada_mods(c_all, w_ada_mix, b_ada_mix), d, db)
    mods_ffn = _Mods(_ada_mods(c_all, w_ada_ffn, b_ada_ffn), d, db)
    mods_kv = _Mods(_ada_mods(c_all, w_ada_kv[None], b_ada_kv[None]), d, db)

    a_re, a_im, wb = _discretize(ssm_lam_re, ssm_lam_im, ssm_log_step, ssm_b_re, ssm_b_im)
    wc = _block_diag_out(ssm_c_re, ssm_c_im)
    w13, w2, wglu = _to_bf16(ffn_w13), _to_bf16(ffn_w2), _to_bf16(ssm_w_glu)
    wkv, wq, wo = _to_bf16(w_kv), _to_bf16(w_q), _to_bf16(w_o)
    row = lambda a: a.reshape(1, -1)
    knw = row(jnp.tile(k_norm.reshape(-1), N_HEADS))
    qnw = [row(jnp.tile(q_norm[j].reshape(-1), N_HEADS)) for j in range(depth - n_a)]
    subw = [row(jnp.tile(subln[j], N_HEADS)) for j in range(depth - n_a)]
    blk = jnp.arange(MXU_DIM) // HEAD_DK
    ones_blk = ((blk[:, None] == blk[None, :]).astype(F32) / HEAD_DK).astype(BF16)
    lam_inits = [0.8 - 0.6 * math.exp(-0.3 * layer) for layer in range(n_a, depth)]

    n_pool, page = cache_k.shape[:2]
    ck = cache_k.transpose(0, 2, 3, 4, 1).reshape(n_pool, nk, page)
    cv = cache_v.reshape(n_pool, page * N_HEADS, HEAD_DV)

    def run_group(x, h0_re, h0_im, nb, row_block, tc, tok_s5, tok_rest, paged):
        turn = tok_rest.kind == "bm"
        new_re, new_im = [], []
        for a in range(n_a):
            last = a == n_a - 1
            x, hr, hi = _s5_layer(x, tc, turn and a == 0, turn and last, mods_mix, a, row_block,
                                  row(norm_mix[a]), h0_re[a], h0_im[a], a_re[a], a_im[a], wb, wc,
                                  row(ssm_d[a]), wglu)
            new_re.append(hr.reshape(nb, g, n))
            new_im.append(hi.reshape(nb, g, n))
            x = _ffn_layer(x, tok_rest if last else tok_s5, mods_ffn, a, row(norm_ffn[a]), w13, w2)
        tok = tok_rest
        k, v, kb, vb = _kv_layer(x, tok, mods_kv, row(norm_kv), wkv, ones_blk, knw)
        for j in range(depth - n_a):
            layer = n_a + j
            if paged:
                q = _query_layer(x, tok, mods_mix, layer, j, row(norm_mix[layer]), wq, ones_blk, qnw[j])
                o = _attn_paged_layer(q, k, v, ck, cv, page_table, diff_lambda[j], lam_inits[j], 8)
                x = _attn_out_layer(x, o, tok, mods_mix, layer, j, subw[j], wo, lam_inits[j])
            else:
                x = _attn_prompt_layer(x, tok, mods_mix, layer, j, row(norm_mix[layer]), wq, ones_blk,
                                       qnw[j], kb, vb, diff_lambda[j], subw[j], wo, lam_inits[j])
            x = _ffn_layer(x, tok, mods_ffn, layer, row(norm_ffn[layer]), w13, w2)
        return x, jnp.stack(new_re), jnp.stack(new_im), k, v

    zeros = jnp.zeros((n_a, pb, gn), F32)
    y_p, re_p, im_p, k_p, v_p = run_group(
        x_prompt, zeros, zeros, pb, db // pb, PROMPT_ROWS // pb,
        _Tokens("tm", pb, seq, d, PROMPT_ROWS, mod_row_block=db // pb),
        _Tokens("bm", pb, seq, d, PROMPT_ROWS), False)

    tok_s = _Tokens("tm", db, dseq, d, db * dseq)
    y_s, re_s, im_s, k_s, v_s = run_group(
        x_sample.transpose(1, 0, 2), state_ssm_re.reshape(n_a, db, gn).astype(F32),
        state_ssm_im.reshape(n_a, db, gn).astype(F32), db, 0, dseq, tok_s, tok_s, True)
    tm2bm = lambda a: a.transpose(1, 0, 2)

    return (y_p, tm2bm(y_s), re_p, im_p, re_s, im_s,
            k_p.reshape(pb, seq, N_HEADS, 2, HEAD_DK), v_p.reshape(pb, seq, N_HEADS, HEAD_DV),
            tm2bm(k_s).reshape(db, dseq, N_HEADS, 2, HEAD_DK), tm2bm(v_s).reshape(db, dseq, N_HEADS, HEAD_DV))
```

```python
import functools
import math

import jax
import jax.numpy as jnp
from jax import lax
from jax.experimental import pallas as pl
from jax.experimental.pallas import tpu as pltpu

F32 = jnp.float32
BF16 = jnp.bfloat16

EPS = 1e-6
N_HEADS = 8
HEAD_DK = 64
HEAD_DV = 2 * HEAD_DK
ATTN_SCALE = HEAD_DK ** -0.5
SSM_GROUP = 16
SSM_STATE = 64
LANES = 128
MXU_DIM = 256
GROUPS_PER_BLOCK = MXU_DIM // SSM_GROUP
VMEM_LIMIT = 56 * 1024 * 1024
NEG_BIG = -1e30
PROMPT_ROWS = 512
PAGES_PER_STEP = 8
LOG2_E = math.log2(math.e)


def _params(n_grid, vmem=VMEM_LIMIT):
    return pltpu.CompilerParams(dimension_semantics=("arbitrary",) * n_grid,
                                vmem_limit_bytes=vmem)


def _const_spec(shape):
    nd = len(shape)
    return pl.BlockSpec(tuple(shape), lambda *_: (0,) * nd, pipeline_mode=pl.Buffered(1))


def _layer_spec(stacked, l):
    nd = stacked.ndim - 1
    return pl.BlockSpec((None,) + stacked.shape[1:], lambda *_: (l,) + (0,) * nd,
                        pipeline_mode=pl.Buffered(1))


def _dot(a, b):
    return jnp.dot(a, b, preferred_element_type=F32)


def _dot_nt(a, b):
    return lax.dot_general(a, b, (((1,), (1,)), ((), ())), preferred_element_type=F32)


def _modulate(x, g, shift, scale):
    ms = jnp.mean(x * x, axis=-1, keepdims=True)
    xn = x * lax.rsqrt(ms + EPS) * g
    return xn * (1.0 + scale) + shift


def _group_rms(y, ones_blk, w):
    parts = []
    for j in range(y.shape[1] // MXU_DIM):
        blk = y[:, j * MXU_DIM:(j + 1) * MXU_DIM]
        parts.append(_dot((blk * blk).astype(BF16), ones_blk))
    ms = jnp.concatenate(parts, axis=1)
    return y * lax.rsqrt(ms + EPS) * w


CAST_BLOCK_BYTES = 6 * 1024 * 1024


def _cast_kernel(w_ref, o_ref):
    o_ref[...] = w_ref[...].astype(o_ref.dtype)


def _to_bf16(w):
    cols = w.shape[-1]
    w2 = w.reshape(-1, cols)
    n_rows = w2.shape[0]
    rb = n_rows
    while rb * cols * 4 > CAST_BLOCK_BYTES and rb % 32 == 0:
        rb //= 2
    out = pl.pallas_call(
        _cast_kernel,
        grid=(n_rows // rb,),
        in_specs=[pl.BlockSpec((rb, cols), lambda i: (i, 0))],
        out_specs=pl.BlockSpec((rb, cols), lambda i: (i, 0)),
        out_shape=jax.ShapeDtypeStruct(w2.shape, BF16),
        compiler_params=_params(1),
        name="cast_bf16",
    )(w2)
    return out.reshape(w.shape)


def _to_bf16_col_chunks(w, n_layers, chunk):
    _, k, n = w.shape
    nc = n // chunk
    return pl.pallas_call(
        _cast_kernel,
        grid=(n_layers, nc),
        in_specs=[pl.BlockSpec((None, k, chunk), lambda l, j: (l, 0, j))],
        out_specs=pl.BlockSpec((None, None, k, chunk), lambda l, j: (l, j, 0, 0)),
        out_shape=jax.ShapeDtypeStruct((n_layers, nc, k, chunk), BF16),
        compiler_params=_params(2),
        name="cast_bf16_chunks",
    )(w)


def _ada_kernel(c_ref, w_ref, b_ref, o_ref):
    c = c_ref[...]
    s = (c * jax.nn.sigmoid(c)).astype(BF16)
    o_ref[...] = _dot(s, w_ref[...].astype(BF16)) + b_ref[...]


def _ada_mods(c_all, w, b):
    nl, d, n = w.shape
    r = c_all.shape[0]
    tn = 1024
    return pl.pallas_call(
        _ada_kernel,
        grid=(nl, n // tn),
        in_specs=[pl.BlockSpec((r, d), lambda l, j: (0, 0)),
                  pl.BlockSpec((None, d, tn), lambda l, j: (l, 0, j)),
                  pl.BlockSpec((None, 1, tn), lambda l, j: (l, 0, j))],
        out_specs=pl.BlockSpec((None, r, tn), lambda l, j: (l, 0, j)),
        out_shape=jax.ShapeDtypeStruct((nl, r, n), F32),
        compiler_params=_params(2),
        name="ada_mods",
    )(c_all, w, b.reshape(nl, 1, n))


class _Mods:
    def __init__(self, arr, d, db):
        self.a3 = arr
        self.a4 = arr.reshape(arr.shape[0], arr.shape[1], 1, arr.shape[2])
        self.d = d
        self.db = db

    def spec_rows(self, l, k, row_block, n_rows):
        return self.a3, pl.BlockSpec((None, n_rows, self.d), lambda *_: (l, row_block, k))

    def spec_prompt_row(self, l, k, tiles_per_seq=1):
        db = self.db
        return self.a4, pl.BlockSpec((None, None, 1, self.d),
                                     lambda b, *_: (l, db + b // tiles_per_seq, 0, k))


def _disc_kernel(lr_ref, li_ref, ls_ref, br_ref, bi_ref, ar_ref, ai_ref, wb_ref, *, n_state):
    dt = jnp.exp(ls_ref[...])
    lr, li = lr_ref[...], li_ref[...]
    mag = jnp.exp(lr * dt)
    a_re, a_im = mag * jnp.cos(li * dt), mag * jnp.sin(li * dt)
    nr, ni = a_re - 1.0, a_im
    den = lr * lr + li * li
    f_re = (nr * lr + ni * li) / den
    f_im = (ni * lr - nr * li) / den
    br, bi = br_ref[...], bi_ref[...]
    ar_ref[...] = a_re
    ai_ref[...] = a_im
    bb_re = f_re * br - f_im * bi
    bb_im = f_re * bi + f_im * br
    nj, kin, two_half = wb_ref.shape
    half = two_half // 2
    p = br.shape[0]
    row = lax.broadcasted_iota(jnp.int32, (kin, half), 0)
    col = lax.broadcasted_iota(jnp.int32, (kin, half), 1)
    own = row // p == col // n_state
    for j in range(nj):
        cols = slice(j * half, (j + 1) * half)
        for part, bb in enumerate((bb_re, bb_im)):
            tiled = jnp.concatenate([bb[:, cols]] * (kin // p), axis=0)
            wb_ref[j, :, part * half:(part + 1) * half] = jnp.where(own, tiled, 0.0).astype(BF16)


def _discretize(lam_re, lam_im, log_step, b_re, b_im):
    na, g, n = lam_re.shape
    p = b_re.shape[-1]
    gn = g * n
    gb = GROUPS_PER_BLOCK
    nj = g // gb
    vec = lambda a: a.reshape(na, 1, gn)
    ls = jnp.broadcast_to(log_step[:, :, None], (na, g, n)).reshape(na, 1, gn)
    bt = lambda a: a.transpose(0, 3, 1, 2).reshape(na, p, gn)
    row = pl.BlockSpec((None, 1, gn), lambda a: (a, 0, 0))
    mat = pl.BlockSpec((None, p, gn), lambda a: (a, 0, 0))
    wb_shape = (na, nj, gb * p, 2 * gb * n)
    return pl.pallas_call(
        functools.partial(_disc_kernel, n_state=n),
        grid=(na,),
        in_specs=[row, row, row, mat, mat],
        out_specs=[row, row, pl.BlockSpec((None,) + wb_shape[1:], lambda a: (a, 0, 0, 0))],
        out_shape=[jax.ShapeDtypeStruct((na, 1, gn), F32)] * 2
        + [jax.ShapeDtypeStruct(wb_shape, BF16)],
        compiler_params=_params(1),
        name="s5_discretize",
    )(vec(lam_re), vec(lam_im), ls, bt(b_re), bt(b_im))


def _block_diag_out(c_re, c_im):
    na, g, p, n = c_re.shape
    gb = GROUPS_PER_BLOCK
    nj = g // gb
    eye = jnp.eye(gb, dtype=F32)

    def one(c):
        c5 = c.reshape(na, nj, gb, p, n)
        w = jnp.einsum('ajgpn,gh->ajgnhp', c5, eye)
        return w.reshape(na, nj, gb * n, gb * p)

    return jnp.concatenate([one(c_re), one(-c_im)], axis=2).astype(BF16)


def _s5_kernel(x_ref, g_ref, sh_ref, sc_ref, gt_ref, h0r_ref, h0i_ref, ar_ref, ai_ref,
               wb_ref, wc_ref, dsk_ref, wglu_ref,
               xo_ref, hfr_ref, hfi_ref,
               str_ref, sti_ref, bu_ref, y_ref, xt_ref, *, col_w, bm_in, bm_out):
    if bm_in:
        nb, tc, d = x_ref.shape
    else:
        tc, nb, d = x_ref.shape
    rows = tc * nb
    nj = wb_ref.shape[0]
    half = wb_ref.shape[2] // 2

    @pl.when(pl.program_id(0) == 0)
    def _():
        str_ref[...] = h0r_ref[...]
        sti_ref[...] = h0i_ref[...]

    lane_blocks = [slice(c * LANES, (c + 1) * LANES) for c in range(d // LANES)]
    if bm_in:
        for b in range(nb):
            for c, cols in enumerate(lane_blocks):
                xt_ref[c, pl.ds(b, tc, stride=nb), :] = x_ref[b, :, cols]
        x = jnp.concatenate([xt_ref[c] for c in range(len(lane_blocks))], axis=1)
        x = x.reshape(tc, nb, d)
    else:
        x = x_ref[...]
    u = _modulate(x, g_ref[...], sh_ref[...], sc_ref[...]).reshape(rows, d)
    ub = u.astype(BF16)

    for j in range(nj):
        kin = wb_ref.shape[1]
        bu_ref[...] = _dot(ub[:, j * kin:(j + 1) * kin], wb_ref[j])
        for cs in range(0, half, col_w):
            g0 = j * half + cs
            ar = jnp.broadcast_to(ar_ref[:, g0:g0 + col_w], (nb, col_w))
            ai = jnp.broadcast_to(ai_ref[:, g0:g0 + col_w], (nb, col_w))

            def step(t, carry, cs=cs, ar=ar, ai=ai):
                hr, hi = carry
                r0 = pl.multiple_of(t * nb, nb)
                bur = bu_ref[pl.ds(r0, nb), cs:cs + col_w]
                bui = bu_ref[pl.ds(r0, nb), half + cs:half + cs + col_w]
                nhr = ar * hr - ai * hi + bur
                nhi = ar * hi + ai * hr + bui
                bu_ref[pl.ds(r0, nb), cs:cs + col_w] = nhr
                bu_ref[pl.ds(r0, nb), half + cs:half + cs + col_w] = nhi
                return nhr, nhi

            hr, hi = lax.fori_loop(0, tc, step,
                                   (str_ref[:, g0:g0 + col_w], sti_ref[:, g0:g0 + col_w]),
                                   unroll=True)
            str_ref[:, g0:g0 + col_w] = hr
            sti_ref[:, g0:g0 + col_w] = hi
        kout = wc_ref.shape[2]
        y_ref[:, j * kout:(j + 1) * kout] = _dot(bu_ref[...].astype(BF16), wc_ref[j])

    y = y_ref[...] + dsk_ref[...] * u
    z = _dot(jax.nn.gelu(y).astype(BF16), wglu_ref[...])
    out = z[:, :d] * jax.nn.sigmoid(z[:, d:])
    res = x + gt_ref[...] * out.reshape(tc, nb, d)
    if bm_out:
        res = res.reshape(rows, d)
        for c, cols in enumerate(lane_blocks):
            xt_ref[c] = res[:, cols]
        for b in range(nb):
            for c, cols in enumerate(lane_blocks):
                xo_ref[b, :, cols] = xt_ref[c, pl.ds(b, tc, stride=nb), :]
    else:
        xo_ref[...] = res
    hfr_ref[...] = str_ref[...]
    hfi_ref[...] = sti_ref[...]


def _s5_layer(x, tc, bm_in, bm_out, mods, l, row_block, norm_g, h0_re, h0_im, a_re, a_im,
              wb, wc, dskip, wglu):
    if bm_in:
        nb, t, d = x.shape
    else:
        t, nb, d = x.shape
    gn = a_re.shape[-1]
    rows = tc * nb
    col_w = max(LANES, (8 * 512) // nb)
    tm_spec = pl.BlockSpec((tc, nb, d), lambda c: (c, 0, 0))
    bm_spec = pl.BlockSpec((nb, tc, d), lambda c: (0, c, 0))
    (m_arr, sh), (_, sc), (_, gt) = (mods.spec_rows(l, k, row_block, nb) for k in range(3))
    st_spec = pl.BlockSpec((nb, gn), lambda c: (0, 0))
    return pl.pallas_call(
        functools.partial(_s5_kernel, col_w=col_w, bm_in=bm_in, bm_out=bm_out),
        grid=(t // tc,),
        in_specs=[bm_spec if bm_in else tm_spec, _const_spec((1, d)), sh, sc, gt, st_spec, st_spec,
                  _const_spec((1, gn)), _const_spec((1, gn)),
                  _layer_spec(wb, l), _layer_spec(wc, l),
                  _const_spec((1, d)), _layer_spec(wglu, l)],
        out_specs=[bm_spec if bm_out else tm_spec, st_spec, st_spec],
        out_shape=[jax.ShapeDtypeStruct((nb, t, d) if bm_out else (t, nb, d), F32),
                   jax.ShapeDtypeStruct((nb, gn), F32), jax.ShapeDtypeStruct((nb, gn), F32)],
        scratch_shapes=[pltpu.VMEM((nb, gn), F32), pltpu.VMEM((nb, gn), F32),
                        pltpu.VMEM((rows, wb.shape[3]), F32), pltpu.VMEM((rows, d), F32),
                        pltpu.VMEM((d // LANES, rows, LANES) if (bm_in or bm_out)
                                   else (1, 8, LANES), F32)],
        compiler_params=_params(1),
        name="s5_layer",
    )(x, norm_g, m_arr, m_arr, m_arr, h0_re, h0_im, a_re, a_im, wb, wc, dskip, wglu)


class _Tokens:
    def __init__(self, kind, nb, t, d, rows, mod_row_block=0):
        self.kind, self.nb, self.t, self.d = kind, nb, t, d
        self.tt = rows // nb if kind == "tm" else rows
        self.mod_row_block = mod_row_block

    @property
    def grid(self):
        return (self.t // self.tt,) if self.kind == "tm" else (self.nb, self.t // self.tt)

    def spec(self, width=None):
        w = self.d if width is None else width
        if self.kind == "tm":
            return pl.BlockSpec((self.tt, self.nb, w), lambda i: (i, 0, 0))
        return pl.BlockSpec((None, self.tt, w), lambda b, i: (b, i, 0))

    def shape(self, dtype=F32, width=None):
        w = self.d if width is None else width
        dims = (self.t, self.nb, w) if self.kind == "tm" else (self.nb, self.t, w)
        return jax.ShapeDtypeStruct(dims, dtype)

    def mods(self, mods, l, n):
        if self.kind == "tm":
            return [mods.spec_rows(l, k, self.mod_row_block, self.nb) for k in range(n)]
        return [mods.spec_prompt_row(l, k) for k in range(n)]

    @property
    def n_tiles(self):
        return math.prod(self.grid)

    def spec_flat(self):
        if self.kind == "tm":
            return pl.BlockSpec((self.tt, self.nb, self.d), lambda s, *_: (s, 0, 0))
        per_seq = self.t // self.tt
        return pl.BlockSpec((None, self.tt, self.d), lambda s, *_: (s // per_seq, s % per_seq, 0))

    def mods_flat(self, mods, l, n):
        if self.kind == "tm":
            return self.mods(mods, l, n)
        return [mods.spec_prompt_row(l, k, tiles_per_seq=self.t // self.tt) for k in range(n)]


def _ffn_chunks(f):
    out, c0 = [], 0
    while c0 < f:
        c1 = min(f, c0 + 1024)
        out.append((c0, c1))
        c0 = c1
    return out


def _ffn_kernel(x_ref, g_ref, sh_ref, sc_ref, gt_ref, w13_ref, w2_ref, o_ref):
    x = x_ref[...]
    d = x.shape[-1]
    f = w2_ref.shape[0]
    h = _modulate(x, g_ref[...], sh_ref[...], sc_ref[...]).reshape(-1, d).astype(BF16)
    acc = None
    for c0, c1 in _ffn_chunks(f):
        a = _dot(h, w13_ref[:, c0:c1])
        b = _dot(h, w13_ref[:, f + c0:f + c1])
        gte = (a * jax.nn.sigmoid(a) * b).astype(BF16)
        part = _dot(gte, w2_ref[c0:c1, :])
        acc = part if acc is None else acc + part
    o_ref[...] = x + gt_ref[...] * acc.reshape(x.shape)


def _ffn_layer(x, tok, mods, l, norm_g, w13, w2):
    (m_arr, sh), (_, sc), (_, gt) = tok.mods(mods, l, 3)
    return pl.pallas_call(
        _ffn_kernel,
        grid=tok.grid,
        in_specs=[tok.spec(), _const_spec((1, tok.d)), sh, sc, gt,
                  _layer_spec(w13, l), _layer_spec(w2, l)],
        out_specs=tok.spec(),
        out_shape=tok.shape(),
        compiler_params=_params(len(tok.grid)),
        name="swiglu_ffn",
    )(x, norm_g, m_arr, m_arr, m_arr, w13, w2)


def _kv_kernel(x_ref, g_ref, sh_ref, sc_ref, wkv_ref, ones_ref, knw_ref,
               k_ref, v_ref, kb_ref, vb_ref):
    x = x_ref[...]
    d = x.shape[-1]
    h = _modulate(x, g_ref[...], sh_ref[...], sc_ref[...]).reshape(-1, d).astype(BF16)
    kv = _dot(h, wkv_ref[...])
    nk = k_ref.shape[-1]
    k = _group_rms(kv[:, :nk], ones_ref[...], knw_ref[...])
    v = kv[:, nk:]
    k_ref[...] = k.reshape(k_ref.shape)
    v_ref[...] = v.reshape(v_ref.shape)
    kb_ref[...] = k.astype(BF16).reshape(kb_ref.shape)
    vb_ref[...] = v.astype(BF16).reshape(vb_ref.shape)


def _kv_layer(x, tok, mods, norm_g, wkv, ones_blk, knw):
    (m_arr, sh), (_, sc) = tok.mods(mods, 0, 2)
    nk = knw.shape[-1]
    nv = wkv.shape[1] - nk
    widths = (nk, nv, nk, nv)
    dtypes = (F32, F32, BF16, BF16)
    return pl.pallas_call(
        _kv_kernel,
        grid=tok.grid,
        in_specs=[tok.spec(), _const_spec((1, tok.d)), sh, sc, _const_spec(wkv.shape),
                  _const_spec(ones_blk.shape), _const_spec(knw.shape)],
        out_specs=[tok.spec(width=w) for w in widths],
        out_shape=[tok.shape(dtype=t, width=w) for w, t in zip(widths, dtypes)],
        compiler_params=_params(len(tok.grid)),
        name="shared_kv",
    )(x, norm_g, m_arr, m_arr, wkv, ones_blk, knw)


def _diff_lambda(lp_ref, lam_init):
    lp = lp_ref[...]
    return (jnp.exp(jnp.sum(lp[0:1] * lp[1:2], axis=-1, keepdims=True))
            - jnp.exp(jnp.sum(lp[2:3] * lp[3:4], axis=-1, keepdims=True)) + lam_init)


def _query(x, g, sh, sc, wq, ones_blk, qnw):
    h = _modulate(x, g, sh, sc).reshape(-1, x.shape[-1]).astype(BF16)
    q = _group_rms(_dot(h, wq), ones_blk, qnw)
    return q * ATTN_SCALE


def _head_out(o, subw, lam_init):
    ms = jnp.mean(o * o, axis=-1, keepdims=True)
    return o * lax.rsqrt(ms + EPS) * subw * (1.0 - lam_init)


def _attn_prompt_kernel(x_ref, g_ref, sh_ref, sc_ref, gt_ref, wq_ref, ones_ref, qnw_ref,
                        kb_ref, vb_ref, lp_ref, subw_ref, wo_ref,
                        o_ref,
                        qs_ref, m_ref, acc_ref, *, lam_init):
    tq, d = x_ref.shape
    tk = tq
    i = pl.program_id(1)
    x = x_ref[...]
    q = _query(x, g_ref[...], sh_ref[...], sc_ref[...], wq_ref[...], ones_ref[...], qnw_ref[...])
    q = q * LOG2_E
    lane = lax.broadcasted_iota(jnp.int32, (tq, HEAD_DV), 1)
    first_map = lane < HEAD_DK
    th = tq // 2
    for h in range(N_HEADS):
        qh = q[:, h * HEAD_DV:(h + 1) * HEAD_DV]
        q0 = jnp.where(first_map, qh, 0.0).astype(BF16)
        q1 = jnp.where(first_map, 0.0, qh).astype(BF16)
        qs_ref[h] = jnp.concatenate([q0[:th], q1[:th], q0[th:], q1[th:]], axis=0)
    m_ref[...] = jnp.full(m_ref.shape, NEG_BIG, F32)
    acc_ref[...] = jnp.zeros(acc_ref.shape, F32)

    def attend(k0, n_keys, r0, n_rows, first_pos):
        for h in range(N_HEADS):
            hs = slice(h * HEAD_DV, (h + 1) * HEAD_DV)
            kh = kb_ref[pl.ds(k0, n_keys), hs]
            vh = vb_ref[pl.ds(k0, n_keys), hs]
            v_ext = jnp.concatenate([vh, jnp.ones_like(vh)], axis=1)
            s = _dot_nt(qs_ref[h, r0:r0 + n_rows, :], kh)
            if first_pos is not None:
                row = lax.broadcasted_iota(jnp.int32, s.shape, 0)
                col = lax.broadcasted_iota(jnp.int32, s.shape, 1)
                s = jnp.where(col <= row % th + first_pos, s, NEG_BIG)
            m_old = m_ref[h, r0:r0 + n_rows, :]
            m_new = jnp.maximum(m_old, jnp.max(s, axis=-1, keepdims=True))
            alpha = jnp.exp2(m_old - m_new)
            p = jnp.exp2(s - jnp.concatenate([m_new] * (n_keys // LANES), axis=1))
            acc_ref[h, r0:r0 + n_rows, :] = (
                jnp.concatenate([alpha, alpha], axis=1) * acc_ref[h, r0:r0 + n_rows, :]
                + _dot(p.astype(BF16), v_ext))
            m_ref[h, r0:r0 + n_rows, :] = m_new

    def body(j, carry):
        attend(pl.multiple_of(j * tk, tk), tk, 0, 2 * tq, None)
        return carry

    lax.fori_loop(0, i, body, 0)
    kd = pl.multiple_of(i * tk, tk)
    attend(kd, th, 0, tq, 0)
    attend(kd, tk, tq, tq, th)

    lam = _diff_lambda(lp_ref, lam_init)
    heads = []
    for h in range(N_HEADS):
        hs = slice(h * HEAD_DV, (h + 1) * HEAD_DV)
        o = acc_ref[h, :, :HEAD_DV] / acc_ref[h, :, HEAD_DV:]
        o0 = jnp.concatenate([o[:th], o[2 * th:3 * th]], axis=0)
        o1 = jnp.concatenate([o[th:2 * th], o[3 * th:]], axis=0)
        heads.append(_head_out(o0 - lam * o1, subw_ref[:, hs], lam_init))
    ob = jnp.concatenate(heads, axis=1).astype(BF16)
    o_ref[...] = x + gt_ref[...] * _dot(ob, wo_ref[...])


def _attn_prompt_layer(x, tok, mods, l, j, norm_g, wq, ones_blk, qnw, kb, vb, lp, subw, wo,
                       lam_init):
    (m_arr, sh), (_, sc), (_, gt) = tok.mods(mods, l, 3)
    seq, d, tq = tok.t, tok.d, tok.tt
    kvspec = pl.BlockSpec((None, seq, d), lambda b, i: (b, 0, 0), pipeline_mode=pl.Buffered(1))
    return pl.pallas_call(
        functools.partial(_attn_prompt_kernel, lam_init=lam_init),
        grid=tok.grid,
        in_specs=[tok.spec(), _const_spec((1, d)), sh, sc, gt, _layer_spec(wq, j),
                  _const_spec(ones_blk.shape), _const_spec(qnw.shape), kvspec, kvspec,
                  _const_spec(lp.shape), _const_spec(subw.shape), _layer_spec(wo, j)],
        out_specs=tok.spec(),
        out_shape=tok.shape(),
        scratch_shapes=[pltpu.VMEM((N_HEADS, 2 * tq, HEAD_DV), BF16),
                        pltpu.VMEM((N_HEADS, 2 * tq, LANES), F32),
                        pltpu.VMEM((N_HEADS, 2 * tq, 2 * HEAD_DV), F32)],
        compiler_params=_params(2),
        name="diff_attn_prompt",
    )(x, norm_g, m_arr, m_arr, m_arr, wq, ones_blk, qnw, kb, vb, lp, subw, wo)


def _query_kernel(x_ref, g_ref, sh_ref, sc_ref, wq_ref, ones_ref, qnw_ref, q_ref):
    q = _query(x_ref[...], g_ref[...], sh_ref[...], sc_ref[...], wq_ref[...], ones_ref[...],
               qnw_ref[...])
    q_ref[...] = q.astype(BF16).reshape(q_ref.shape)


def _query_layer(x, tok, mods, l, j, norm_g, wq, ones_blk, qnw):
    (m_arr, sh), (_, sc) = tok.mods(mods, l, 2)
    return pl.pallas_call(
        _query_kernel,
        grid=tok.grid,
        in_specs=[tok.spec(), _const_spec((1, tok.d)), sh, sc, _layer_spec(wq, j),
                  _const_spec(ones_blk.shape), _const_spec(qnw.shape)],
        out_specs=tok.spec(),
        out_shape=tok.shape(dtype=BF16),
        compiler_params=_params(len(tok.grid)),
        name="diff_attn_query",
    )(x, norm_g, m_arr, m_arr, wq, ones_blk, qnw)


def _attn_paged_kernel(pt_ref, q_ref, kn_ref, vn_ref, lp_ref, *rest, n_pages, lam_init):
    k_refs = rest[:n_pages]
    v_refs = rest[n_pages:2 * n_pages]
    (x_ref, g_ref, sh_ref, sc_ref, gt_ref, w13c_ref, w2c_ref, o_ref, xo_ref,
     qx_ref, m_ref, l_ref, acc_ref, h_ref, facc_ref) = rest[2 * n_pages:]
    c = pl.program_id(1)
    n_steps = pl.num_programs(1)
    t_new, d = q_ref.shape
    page = k_refs[0].shape[1]

    rows_per_head = 2 * t_new

    n_cols = w2c_ref.shape[0]
    x = x_ref[...]

    def ffn_chunk(ci):
        h = h_ref[...]
        a = _dot(h, w13c_ref[ci])
        b = _dot(h, w13c_ref[n_cols + ci])
        return _dot((a * jax.nn.sigmoid(a) * b).astype(BF16), w2c_ref[ci])

    @pl.when(c == 0)
    def _():
        h_ref[...] = _modulate(x, g_ref[...], sh_ref[...], sc_ref[...]).reshape(-1, d).astype(BF16)
        facc_ref[...] = jnp.zeros(facc_ref.shape, F32)
        qt = jnp.concatenate([q_ref[...].astype(F32)] * (2 * N_HEADS), axis=0)
        row = lax.broadcasted_iota(jnp.int32, qt.shape, 0)
        col = lax.broadcasted_iota(jnp.int32, qt.shape, 1)
        qx_ref[...] = jnp.where(row // t_new == col // HEAD_DK, qt, 0.0).astype(BF16)
        m_ref[...] = jnp.full(m_ref.shape, NEG_BIG, F32)
        l_ref[...] = jnp.zeros(l_ref.shape, F32)
        acc_ref[...] = jnp.zeros(acc_ref.shape, F32)

    def scores_update(s):
        m_old = m_ref[...]
        m_new = jnp.maximum(m_old, jnp.max(s, axis=-1, keepdims=True))
        alpha = jnp.exp(m_old - m_new)
        p = jnp.exp(s - jnp.concatenate([m_new] * (s.shape[1] // LANES), axis=1))
        l_ref[...] = alpha * l_ref[...] + jnp.sum(p, axis=-1, keepdims=True)
        m_ref[...] = m_new
        return alpha, p.astype(BF16)

    def values_update(alpha, pb, v_of_head):
        for h in range(N_HEADS):
            rs = slice(h * rows_per_head, (h + 1) * rows_per_head)
            acc_ref[rs, :] = alpha[rs] * acc_ref[rs, :] + _dot(pb[rs], v_of_head(h))

    def cached_v(h):
        return jnp.concatenate(
            [r[pl.ds(h, page, stride=N_HEADS), :].astype(BF16) for r in v_refs], axis=0)

    kt = jnp.concatenate([r[...].astype(BF16) for r in k_refs], axis=1)
    heads_per_slab = MXU_DIM // HEAD_DV
    slab_rows = heads_per_slab * rows_per_head
    s = jnp.concatenate(
        [_dot(qx_ref[g * slab_rows:(g + 1) * slab_rows, g * MXU_DIM:(g + 1) * MXU_DIM],
              kt[g * MXU_DIM:(g + 1) * MXU_DIM, :]) for g in range(d // MXU_DIM)], axis=0)
    alpha, pb = scores_update(s)
    values_update(alpha, pb, cached_v)
    facc_ref[...] += ffn_chunk(c)

    @pl.when(c + n_steps < n_cols)
    def _():
        facc_ref[...] += ffn_chunk(c + n_steps)

    @pl.when(c == n_steps - 1)
    def _():
        xo_ref[...] = x + gt_ref[...] * facc_ref[...].reshape(x.shape)
        pad = jnp.zeros((page - t_new, d), F32)
        kb_new = jnp.concatenate([kn_ref[...], pad], axis=0).astype(BF16)
        vb_new = jnp.concatenate([vn_ref[...], pad], axis=0).astype(BF16)
        s = _dot_nt(qx_ref[...], kb_new)
        row = lax.broadcasted_iota(jnp.int32, s.shape, 0)
        col = lax.broadcasted_iota(jnp.int32, s.shape, 1)
        s = jnp.where(col <= row % t_new, s, NEG_BIG)
        alpha_new, pb_new = scores_update(s)
        values_update(alpha_new, pb_new, lambda h: vb_new[:, h * HEAD_DV:(h + 1) * HEAD_DV])
        lam = _diff_lambda(lp_ref, lam_init)
        inv_l = 1.0 / l_ref[...]
        for h in range(N_HEADS):
            r0 = h * rows_per_head
            o0 = acc_ref[r0:r0 + t_new, :] * inv_l[r0:r0 + t_new]
            o1 = acc_ref[r0 + t_new:r0 + 2 * t_new, :] * inv_l[r0 + t_new:r0 + 2 * t_new]
            o_ref[:, h * HEAD_DV:(h + 1) * HEAD_DV] = o0 - lam * o1


def _attn_paged_ffn_layer(q_tm, kn_tm, vn_tm, cache_kt, cache_vr, page_table, lp, lam_init, n_pages,
                          x, tok, mods, l, norm_g, w13c, w2c):
    t_new, nb, d = q_tm.shape
    page = cache_kt.shape[2]
    n_chunks = page_table.shape[1] // n_pages
    n_rows = 2 * N_HEADS * t_new
    assert tok.n_tiles == nb, "one FFN row tile rides with each sample sequence"
    assert w2c.shape[1] <= 2 * n_chunks, "at most two hidden-column chunks per step"
    ffn_rows = tok.tt * (tok.nb if tok.kind == "tm" else 1)
    row_spec = pl.BlockSpec((t_new, d), lambda b, c, pt: (0, b))

    def page_spec(p, shape):
        return pl.BlockSpec((None,) + shape, lambda b, c, pt: (pt[b, c * n_pages + p], 0, 0))

    k_pages = [page_spec(p, cache_kt.shape[1:]) for p in range(n_pages)]
    v_pages = [page_spec(p, cache_vr.shape[1:]) for p in range(n_pages)]
    (m_arr, sh), (_, sc), (_, gt) = tok.mods_flat(mods, l, 3)
    grid_spec = pltpu.PrefetchScalarGridSpec(
        num_scalar_prefetch=1,
        grid=(nb, n_chunks),
        in_specs=[row_spec, row_spec, row_spec,
                  pl.BlockSpec(lp.shape, lambda b, c, pt: (0, 0))] + k_pages + v_pages
        + [tok.spec_flat(), _const_spec((1, d)), sh, sc, gt, _layer_spec(w13c, l), _layer_spec(w2c, l)],
        out_specs=[row_spec, tok.spec_flat()],
        scratch_shapes=[pltpu.VMEM((n_rows, d), BF16), pltpu.VMEM((n_rows, LANES), F32),
                        pltpu.VMEM((n_rows, LANES), F32), pltpu.VMEM((n_rows, HEAD_DV), F32),
                        pltpu.VMEM((ffn_rows, d), BF16), pltpu.VMEM((ffn_rows, d), F32)],
    )
    flat = lambda a: a.reshape(t_new, nb * d)
    o, x_new = pl.pallas_call(
        functools.partial(_attn_paged_kernel, n_pages=n_pages, lam_init=lam_init),
        grid_spec=grid_spec,
        out_shape=[jax.ShapeDtypeStruct((t_new, nb * d), F32), tok.shape()],
        compiler_params=_params(2),
        name="diff_attn_paged_ffn",
    )(page_table, flat(q_tm), flat(kn_tm), flat(vn_tm), lp,
      *([cache_kt] * n_pages), *([cache_vr] * n_pages),
      x, norm_g, m_arr, m_arr, m_arr, w13c, w2c)
    return o.reshape(t_new, nb, d), x_new


def _attn_out_kernel(x_ref, o_ref_in, gt_ref, subw_ref, wo_ref, out_ref, *, lam_init):
    x = x_ref[...]
    d = x.shape[-1]
    o = o_ref_in[...].reshape(-1, d)
    heads = []
    for h in range(N_HEADS):
        hs = slice(h * HEAD_DV, (h + 1) * HEAD_DV)
        heads.append(_head_out(o[:, hs], subw_ref[:, hs], lam_init))
    ob = jnp.concatenate(heads, axis=1).astype(BF16)
    out_ref[...] = x + gt_ref[...] * _dot(ob, wo_ref[...]).reshape(x.shape)


def _attn_out_layer(x, o, tok, mods, l, j, subw, wo, lam_init):
    m_arr, gt = tok.mods(mods, l, 3)[2]
    return pl.pallas_call(
        functools.partial(_attn_out_kernel, lam_init=lam_init),
        grid=tok.grid,
        in_specs=[tok.spec(), tok.spec(), gt, _const_spec(subw.shape), _layer_spec(wo, j)],
        out_specs=tok.spec(),
        out_shape=tok.shape(),
        compiler_params=_params(len(tok.grid)),
        name="diff_attn_out",
    )(x, o, m_arr, subw, wo)


def kernel(x_prompt, x_sample, state_ssm_re, state_ssm_im, cache_k, cache_v, page_table, c_prompt, c_sample, norm_mix, w_ada_mix, b_ada_mix, norm_ffn, w_ada_ffn, b_ada_ffn, ffn_w13, ffn_w2, ssm_lam_re, ssm_lam_im, ssm_log_step, ssm_b_re, ssm_b_im, ssm_c_re, ssm_c_im, ssm_d, ssm_w_glu, norm_kv, w_ada_kv, b_ada_kv, w_kv, k_norm, w_q, q_norm, diff_lambda, subln, w_o):
    pb, seq, d = x_prompt.shape
    db, dseq, _ = x_sample.shape
    depth = norm_mix.shape[0]
    n_a, g, n = ssm_lam_re.shape
    gn = g * n
    nk = N_HEADS * 2 * HEAD_DK

    c_all = jnp.concatenate([c_sample, c_prompt], axis=0)
    mods_mix = _Mods(_ada_mods(c_all, w_ada_mix, b_ada_mix), d, db)
    mods_ffn = _Mods(_ada_mods(c_all, w_ada_ffn, b_ada_ffn), d, db)
    mods_kv = _Mods(_ada_mods(c_all, w_ada_kv[None], b_ada_kv[None]), d, db)

    a_re, a_im, wb = _discretize(ssm_lam_re, ssm_lam_im, ssm_log_step, ssm_b_re, ssm_b_im)
    wc = _block_diag_out(ssm_c_re, ssm_c_im)
    w13, w2, wglu = _to_bf16(ffn_w13), _to_bf16(ffn_w2), _to_bf16(ssm_w_glu)
    wkv, wq, wo = _to_bf16(w_kv), _to_bf16(w_q), _to_bf16(w_o)
    w13c = _to_bf16_col_chunks(ffn_w13, n_a, MXU_DIM)
    w2c = w2.reshape(depth, -1, MXU_DIM, d)
    row = lambda a: a.reshape(1, -1)
    knw = row(jnp.tile(k_norm.reshape(-1), N_HEADS))
    qnw = [row(jnp.tile(q_norm[j].reshape(-1), N_HEADS)) for j in range(depth - n_a)]
    subw = [row(jnp.tile(subln[j], N_HEADS)) for j in range(depth - n_a)]
    blk = jnp.arange(MXU_DIM) // HEAD_DK
    ones_blk = ((blk[:, None] == blk[None, :]).astype(F32) / HEAD_DK).astype(BF16)
    lam_inits = [0.8 - 0.6 * math.exp(-0.3 * layer) for layer in range(n_a, depth)]

    n_pool, page = cache_k.shape[:2]
    ck = cache_k.transpose(0, 2, 3, 4, 1).reshape(n_pool, nk, page)
    cv = cache_v.reshape(n_pool, page * N_HEADS, HEAD_DV)

    n_b = depth - n_a
    assert n_b <= n_a, "every paged attention layer needs an S5-phase FFN of the prompt group to host"

    def s5(x, a, tc, bm_in, bm_out, row_block, h0_re, h0_im):
        return _s5_layer(x, tc, bm_in, bm_out, mods_mix, a, row_block, row(norm_mix[a]),
                         h0_re[a], h0_im[a], a_re[a], a_im[a], wb, wc, row(ssm_d[a]), wglu)

    def ffn(x, tok, layer):
        return _ffn_layer(x, tok, mods_ffn, layer, row(norm_ffn[layer]), w13, w2)

    def states(parts, nb):
        return jnp.stack([p.reshape(nb, g, n) for p in parts])

    tok_s = _Tokens("tm", db, dseq, d, db * dseq)
    xs = x_sample.transpose(1, 0, 2)
    hs_re, hs_im = state_ssm_re.reshape(n_a, db, gn).astype(F32), state_ssm_im.reshape(n_a, db, gn).astype(F32)
    re_s, im_s = [], []
    for a in range(n_a):
        xs, hr, hi = s5(xs, a, dseq, False, False, 0, hs_re, hs_im)
        re_s.append(hr)
        im_s.append(hi)
        xs = ffn(xs, tok_s, a)
    k_s, v_s, _, _ = _kv_layer(xs, tok_s, mods_kv, row(norm_kv), wkv, ones_blk, knw)

    tok_tm = _Tokens("tm", pb, seq, d, PROMPT_ROWS, mod_row_block=db // pb)
    tok_bm = _Tokens("bm", pb, seq, d, PROMPT_ROWS)
    zeros = jnp.zeros((n_a, pb, gn), F32)
    xp = x_prompt
    re_p, im_p = [], []
    for a in range(n_a):
        last = a == n_a - 1
        xp, hr, hi = s5(xp, a, PROMPT_ROWS // pb, a == 0, last, db // pb, zeros, zeros)
        re_p.append(hr)
        im_p.append(hi)
        tok = tok_bm if last else tok_tm
        if a < n_b:
            j, layer = a, n_a + a
            q = _query_layer(xs, tok_s, mods_mix, layer, j, row(norm_mix[layer]), wq, ones_blk, qnw[j])
            o, xp = _attn_paged_ffn_layer(q, k_s, v_s, ck, cv, page_table, diff_lambda[j], lam_inits[j],
                                          PAGES_PER_STEP, xp, tok, mods_ffn, a, row(norm_ffn[a]),
                                          w13c, w2c)
            xs = _attn_out_layer(xs, o, tok_s, mods_mix, layer, j, subw[j], wo, lam_inits[j])
            xs = ffn(xs, tok_s, layer)
        else:
            xp = ffn(xp, tok, a)

    k_p, v_p, kb, vb = _kv_layer(xp, tok_bm, mods_kv, row(norm_kv), wkv, ones_blk, knw)
    for j in range(n_b):
        layer = n_a + j
        xp = _attn_prompt_layer(xp, tok_bm, mods_mix, layer, j, row(norm_mix[layer]), wq, ones_blk,
                                qnw[j], kb, vb, diff_lambda[j], subw[j], wo, lam_inits[j])
        xp = ffn(xp, tok_bm, layer)

    y_p, y_s = xp, xs
    re_p, im_p, re_s, im_s = states(re_p, pb), states(im_p, pb), states(re_s, db), states(im_s, db)
    tm2bm = lambda a: a.transpose(1, 0, 2)

    return (y_p, tm2bm(y_s), re_p, im_p, re_s, im_s,
            k_p.reshape(pb, seq, N_HEADS, 2, HEAD_DK), v_p.reshape(pb, seq, N_HEADS, HEAD_DV),
            tm2bm(k_s).reshape(db, dseq, N_HEADS, 2, HEAD_DK), tm2bm(v_s).reshape(db, dseq, N_HEADS, HEAD_DV))
```

```python
import functools
import math

import jax
import jax.numpy as jnp
from jax import lax
from jax.experimental import pallas as pl
from jax.experimental.pallas import tpu as pltpu

F32 = jnp.float32
BF16 = jnp.bfloat16

EPS = 1e-6
N_HEADS = 8
HEAD_DK = 64
HEAD_DV = 2 * HEAD_DK
ATTN_SCALE = HEAD_DK ** -0.5
SSM_GROUP = 16
LANES = 128
MXU_DIM = 256
GROUPS_PER_BLOCK = MXU_DIM // SSM_GROUP
VMEM_LIMIT = 56 * 1024 * 1024
NEG_BIG = -1e30
PROMPT_ROWS = 512
PAGES_PER_STEP = 8
LOG2_E = math.log2(math.e)


def _params(n_grid, vmem=VMEM_LIMIT):
    return pltpu.CompilerParams(dimension_semantics=("arbitrary",) * n_grid,
                                vmem_limit_bytes=vmem)


def _const_spec(shape):
    nd = len(shape)
    return pl.BlockSpec(tuple(shape), lambda *_: (0,) * nd, pipeline_mode=pl.Buffered(1))


def _layer_spec(stacked, l):
    nd = stacked.ndim - 1
    return pl.BlockSpec((None,) + stacked.shape[1:], lambda *_: (l,) + (0,) * nd,
                        pipeline_mode=pl.Buffered(1))


def _dot(a, b):
    return jnp.dot(a, b, preferred_element_type=F32)


def _dot_nt(a, b):
    return lax.dot_general(a, b, (((1,), (1,)), ((), ())), preferred_element_type=F32)


def _modulate(x, g, shift, scale):
    ms = jnp.mean(x * x, axis=-1, keepdims=True)
    xn = x * lax.rsqrt(ms + EPS) * g
    return xn * (1.0 + scale) + shift


def _group_rms(y, ones_blk, w):
    parts = []
    for j in range(y.shape[1] // MXU_DIM):
        blk = y[:, j * MXU_DIM:(j + 1) * MXU_DIM]
        parts.append(_dot((blk * blk).astype(BF16), ones_blk))
    ms = jnp.concatenate(parts, axis=1)
    return y * lax.rsqrt(ms + EPS) * w


CAST_BLOCK_BYTES = 6 * 1024 * 1024


def _cast_kernel(w_ref, o_ref):
    o_ref[...] = w_ref[...].astype(o_ref.dtype)


def _to_bf16(w):
    cols = w.shape[-1]
    w2 = w.reshape(-1, cols)
    n_rows = w2.shape[0]
    rb = n_rows
    while rb * cols * 4 > CAST_BLOCK_BYTES and rb % 32 == 0:
        rb //= 2
    out = pl.pallas_call(
        _cast_kernel,
        grid=(n_rows // rb,),
        in_specs=[pl.BlockSpec((rb, cols), lambda i: (i, 0))],
        out_specs=pl.BlockSpec((rb, cols), lambda i: (i, 0)),
        out_shape=jax.ShapeDtypeStruct(w2.shape, BF16),
        compiler_params=_params(1),
        name="cast_bf16",
    )(w2)
    return out.reshape(w.shape)


def _cast_chunks_kernel(w_ref, o_ref):
    n_chunks, _, chunk = o_ref.shape
    for j in range(n_chunks):
        o_ref[j] = w_ref[:, j * chunk:(j + 1) * chunk].astype(o_ref.dtype)


def _to_bf16_col_chunks(w, chunk):
    n_layers, k, n = w.shape
    per_step = n // (2 * chunk)
    return pl.pallas_call(
        _cast_chunks_kernel,
        grid=(n_layers, 2),
        in_specs=[pl.BlockSpec((None, k, n // 2), lambda l, j: (l, 0, j))],
        out_specs=pl.BlockSpec((None, per_step, k, chunk), lambda l, j: (l, j, 0, 0)),
        out_shape=jax.ShapeDtypeStruct((n_layers, n // chunk, k, chunk), BF16),
        compiler_params=_params(2),
        name="cast_bf16_chunks",
    )(w)


def _ada_kernel(c_ref, w_ref, b_ref, o_ref):
    c = c_ref[...]
    s = (c * jax.nn.sigmoid(c)).astype(BF16)
    o_ref[...] = _dot(s, w_ref[...].astype(BF16)) + b_ref[...]


def _ada_mods(c_all, w, b):
    nl, d, n = w.shape
    r = c_all.shape[0]
    tn = 1024
    return pl.pallas_call(
        _ada_kernel,
        grid=(nl, n // tn),
        in_specs=[pl.BlockSpec((r, d), lambda l, j: (0, 0)),
                  pl.BlockSpec((None, d, tn), lambda l, j: (l, 0, j)),
                  pl.BlockSpec((None, 1, tn), lambda l, j: (l, 0, j))],
        out_specs=pl.BlockSpec((None, r, tn), lambda l, j: (l, 0, j)),
        out_shape=jax.ShapeDtypeStruct((nl, r, n), F32),
        compiler_params=_params(2),
        name="ada_mods",
    )(c_all, w, b.reshape(nl, 1, n))


class _Mods:
    def __init__(self, arr, d, db):
        self.a3 = arr
        self.a4 = arr.reshape(arr.shape[0], arr.shape[1], 1, arr.shape[2])
        self.d = d
        self.db = db

    def spec_rows(self, l, k, row_block, n_rows):
        return self.a3, pl.BlockSpec((None, n_rows, self.d), lambda *_: (l, row_block, k))

    def spec_prompt_row(self, l, k, tiles_per_seq=1):
        db = self.db
        return self.a4, pl.BlockSpec((None, None, 1, self.d),
                                     lambda b, *_: (l, db + b // tiles_per_seq, 0, k))


def _disc_kernel(lr_ref, li_ref, ls_ref, br_ref, bi_ref, ar_ref, ai_ref, wb_ref, *, n_state):
    dt = jnp.exp(ls_ref[...])
    lr, li = lr_ref[...], li_ref[...]
    mag = jnp.exp(lr * dt)
    a_re, a_im = mag * jnp.cos(li * dt), mag * jnp.sin(li * dt)
    nr, ni = a_re - 1.0, a_im
    den = lr * lr + li * li
    f_re = (nr * lr + ni * li) / den
    f_im = (ni * lr - nr * li) / den
    br, bi = br_ref[...], bi_ref[...]
    ar_ref[...] = a_re
    ai_ref[...] = a_im
    bb_re = f_re * br - f_im * bi
    bb_im = f_re * bi + f_im * br
    nj, kin, two_half = wb_ref.shape
    half = two_half // 2
    p = br.shape[0]
    row = lax.broadcasted_iota(jnp.int32, (kin, half), 0)
    col = lax.broadcasted_iota(jnp.int32, (kin, half), 1)
    own = row // p == col // n_state
    for j in range(nj):
        cols = slice(j * half, (j + 1) * half)
        for part, bb in enumerate((bb_re, bb_im)):
            tiled = jnp.concatenate([bb[:, cols]] * (kin // p), axis=0)
            wb_ref[j, :, part * half:(part + 1) * half] = jnp.where(own, tiled, 0.0).astype(BF16)


def _discretize(lam_re, lam_im, log_step, b_re, b_im):
    na, g, n = lam_re.shape
    p = b_re.shape[-1]
    gn = g * n
    gb = GROUPS_PER_BLOCK
    nj = g // gb
    vec = lambda a: a.reshape(na, 1, gn)
    ls = jnp.broadcast_to(log_step[:, :, None], (na, g, n)).reshape(na, 1, gn)
    bt = lambda a: a.transpose(0, 3, 1, 2).reshape(na, p, gn)
    row = pl.BlockSpec((None, 1, gn), lambda a: (a, 0, 0))
    mat = pl.BlockSpec((None, p, gn), lambda a: (a, 0, 0))
    wb_shape = (na, nj, gb * p, 2 * gb * n)
    return pl.pallas_call(
        functools.partial(_disc_kernel, n_state=n),
        grid=(na,),
        in_specs=[row, row, row, mat, mat],
        out_specs=[row, row, pl.BlockSpec((None,) + wb_shape[1:], lambda a: (a, 0, 0, 0))],
        out_shape=[jax.ShapeDtypeStruct((na, 1, gn), F32)] * 2
        + [jax.ShapeDtypeStruct(wb_shape, BF16)],
        compiler_params=_params(1),
        name="s5_discretize",
    )(vec(lam_re), vec(lam_im), ls, bt(b_re), bt(b_im))


def _block_diag_out(c_re, c_im):
    na, g, p, n = c_re.shape
    gb = GROUPS_PER_BLOCK
    nj = g // gb
    eye = jnp.eye(gb, dtype=F32)

    def one(c):
        c5 = c.reshape(na, nj, gb, p, n)
        w = jnp.einsum('ajgpn,gh->ajgnhp', c5, eye)
        return w.reshape(na, nj, gb * n, gb * p)

    return jnp.concatenate([one(c_re), one(-c_im)], axis=2).astype(BF16)


def _s5_kernel(x_ref, g_ref, sh_ref, sc_ref, gt_ref, h0r_ref, h0i_ref, ar_ref, ai_ref,
               wb_ref, wc_ref, dsk_ref, wglu_ref,
               xo_ref, hfr_ref, hfi_ref,
               str_ref, sti_ref, bu_ref, y_ref, xt_ref, *, col_w, bm_in, bm_out):
    if bm_in:
        nb, tc, d = x_ref.shape
    else:
        tc, nb, d = x_ref.shape
    rows = tc * nb
    nj = wb_ref.shape[0]
    half = wb_ref.shape[2] // 2

    @pl.when(pl.program_id(0) == 0)
    def _():
        str_ref[...] = h0r_ref[...]
        sti_ref[...] = h0i_ref[...]

    lane_blocks = [slice(c * LANES, (c + 1) * LANES) for c in range(d // LANES)]
    if bm_in:
        for b in range(nb):
            for c, cols in enumerate(lane_blocks):
                xt_ref[c, pl.ds(b, tc, stride=nb), :] = x_ref[b, :, cols]
        x = jnp.concatenate([xt_ref[c] for c in range(len(lane_blocks))], axis=1)
        x = x.reshape(tc, nb, d)
    else:
        x = x_ref[...]
    u = _modulate(x, g_ref[...], sh_ref[...], sc_ref[...]).reshape(rows, d)
    ub = u.astype(BF16)

    for j in range(nj):
        kin = wb_ref.shape[1]
        bu_ref[...] = _dot(ub[:, j * kin:(j + 1) * kin], wb_ref[j])
        for cs in range(0, half, col_w):
            g0 = j * half + cs
            ar = jnp.broadcast_to(ar_ref[:, g0:g0 + col_w], (nb, col_w))
            ai = jnp.broadcast_to(ai_ref[:, g0:g0 + col_w], (nb, col_w))

            def step(t, carry, cs=cs, ar=ar, ai=ai):
                hr, hi = carry
                r0 = pl.multiple_of(t * nb, nb)
                bur = bu_ref[pl.ds(r0, nb), cs:cs + col_w]
                bui = bu_ref[pl.ds(r0, nb), half + cs:half + cs + col_w]
                nhr = ar * hr - ai * hi + bur
                nhi = ar * hi + ai * hr + bui
                bu_ref[pl.ds(r0, nb), cs:cs + col_w] = nhr
                bu_ref[pl.ds(r0, nb), half + cs:half + cs + col_w] = nhi
                return nhr, nhi

            hr, hi = lax.fori_loop(0, tc, step,
                                   (str_ref[:, g0:g0 + col_w], sti_ref[:, g0:g0 + col_w]),
                                   unroll=True)
            str_ref[:, g0:g0 + col_w] = hr
            sti_ref[:, g0:g0 + col_w] = hi
        kout = wc_ref.shape[2]
        y_ref[:, j * kout:(j + 1) * kout] = _dot(bu_ref[...].astype(BF16), wc_ref[j])

    y = y_ref[...] + dsk_ref[...] * u
    z = _dot(jax.nn.gelu(y).astype(BF16), wglu_ref[...])
    out = z[:, :d] * jax.nn.sigmoid(z[:, d:])
    res = x + gt_ref[...] * out.reshape(tc, nb, d)
    if bm_out:
        res = res.reshape(rows, d)
        for c, cols in enumerate(lane_blocks):
            xt_ref[c] = res[:, cols]
        for b in range(nb):
            for c, cols in enumerate(lane_blocks):
                xo_ref[b, :, cols] = xt_ref[c, pl.ds(b, tc, stride=nb), :]
    else:
        xo_ref[...] = res
    hfr_ref[...] = str_ref[...]
    hfi_ref[...] = sti_ref[...]


def _s5_layer(x, tc, bm_in, bm_out, mods, l, row_block, norm_g, h0_re, h0_im, a_re, a_im,
              wb, wc, dskip, wglu):
    if bm_in:
        nb, t, d = x.shape
    else:
        t, nb, d = x.shape
    gn = a_re.shape[-1]
    rows = tc * nb
    col_w = max(LANES, (8 * 512) // nb)
    tm_spec = pl.BlockSpec((tc, nb, d), lambda c: (c, 0, 0))
    bm_spec = pl.BlockSpec((nb, tc, d), lambda c: (0, c, 0))
    (m_arr, sh), (_, sc), (_, gt) = (mods.spec_rows(l, k, row_block, nb) for k in range(3))
    st_spec = pl.BlockSpec((nb, gn), lambda c: (0, 0))
    return pl.pallas_call(
        functools.partial(_s5_kernel, col_w=col_w, bm_in=bm_in, bm_out=bm_out),
        grid=(t // tc,),
        in_specs=[bm_spec if bm_in else tm_spec, _const_spec((1, d)), sh, sc, gt, st_spec, st_spec,
                  _const_spec((1, gn)), _const_spec((1, gn)),
                  _layer_spec(wb, l), _layer_spec(wc, l),
                  _const_spec((1, d)), _layer_spec(wglu, l)],
        out_specs=[bm_spec if bm_out else tm_spec, st_spec, st_spec],
        out_shape=[jax.ShapeDtypeStruct((nb, t, d) if bm_out else (t, nb, d), F32),
                   jax.ShapeDtypeStruct((nb, gn), F32), jax.ShapeDtypeStruct((nb, gn), F32)],
        scratch_shapes=[pltpu.VMEM((nb, gn), F32), pltpu.VMEM((nb, gn), F32),
                        pltpu.VMEM((rows, wb.shape[3]), F32), pltpu.VMEM((rows, d), F32),
                        pltpu.VMEM((d // LANES, rows, LANES) if (bm_in or bm_out)
                                   else (1, 8, LANES), F32)],
        compiler_params=_params(1),
        name="s5_layer",
    )(x, norm_g, m_arr, m_arr, m_arr, h0_re, h0_im, a_re, a_im, wb, wc, dskip, wglu)


class _Tokens:
    def __init__(self, kind, nb, t, d, rows, mod_row_block=0):
        self.kind, self.nb, self.t, self.d = kind, nb, t, d
        self.tt = rows // nb if kind == "tm" else rows
        self.mod_row_block = mod_row_block

    @property
    def grid(self):
        return (self.t // self.tt,) if self.kind == "tm" else (self.nb, self.t // self.tt)

    def spec(self, width=None):
        w = self.d if width is None else width
        if self.kind == "tm":
            return pl.BlockSpec((self.tt, self.nb, w), lambda i: (i, 0, 0))
        return pl.BlockSpec((None, self.tt, w), lambda b, i: (b, i, 0))

    def shape(self, dtype=F32, width=None):
        w = self.d if width is None else width
        dims = (self.t, self.nb, w) if self.kind == "tm" else (self.nb, self.t, w)
        return jax.ShapeDtypeStruct(dims, dtype)

    def mods(self, mods, l, n):
        if self.kind == "tm":
            return [mods.spec_rows(l, k, self.mod_row_block, self.nb) for k in range(n)]
        return [mods.spec_prompt_row(l, k) for k in range(n)]

    @property
    def n_tiles(self):
        return math.prod(self.grid)

    def spec_flat(self):
        if self.kind == "tm":
            return pl.BlockSpec((self.tt, self.nb, self.d), lambda s, *_: (s, 0, 0))
        per_seq = self.t // self.tt
        return pl.BlockSpec((None, self.tt, self.d), lambda s, *_: (s // per_seq, s % per_seq, 0))

    def mods_flat(self, mods, l, n):
        if self.kind == "tm":
            return self.mods(mods, l, n)
        return [mods.spec_prompt_row(l, k, tiles_per_seq=self.t // self.tt) for k in range(n)]


FFN_GROUP = 4


def _ffn_kernel(x_ref, g_ref, sh_ref, sc_ref, gt_ref, w13c_ref, w2c_ref, o_ref):
    x = x_ref[...]
    d = x.shape[-1]
    n_cols, cw, _ = w2c_ref.shape
    h = _modulate(x, g_ref[...], sh_ref[...], sc_ref[...]).reshape(-1, d).astype(BF16)
    acc = None
    for c0 in range(0, n_cols, FFN_GROUP):
        cs = range(c0, min(n_cols, c0 + FFN_GROUP))
        a = jnp.concatenate([_dot(h, w13c_ref[ci]) for ci in cs], axis=1)
        b = jnp.concatenate([_dot(h, w13c_ref[n_cols + ci]) for ci in cs], axis=1)
        gte = (a * jax.nn.sigmoid(a) * b).astype(BF16)
        part = _dot(gte, w2c_ref[c0:c0 + len(cs)].reshape(len(cs) * cw, d))
        acc = part if acc is None else acc + part
    o_ref[...] = x + gt_ref[...] * acc.reshape(x.shape)


def _ffn_layer(x, tok, mods, l, norm_g, w13c, w2c):
    (m_arr, sh), (_, sc), (_, gt) = tok.mods(mods, l, 3)
    return pl.pallas_call(
        _ffn_kernel,
        grid=tok.grid,
        in_specs=[tok.spec(), _const_spec((1, tok.d)), sh, sc, gt,
                  _layer_spec(w13c, l), _layer_spec(w2c, l)],
        out_specs=tok.spec(),
        out_shape=tok.shape(),
        compiler_params=_params(len(tok.grid)),
        name="swiglu_ffn",
    )(x, norm_g, m_arr, m_arr, m_arr, w13c, w2c)


def _kv_kernel(x_ref, g_ref, sh_ref, sc_ref, wkv_ref, ones_ref, knw_ref,
               k_ref, v_ref, kb_ref, vb_ref):
    x = x_ref[...]
    d = x.shape[-1]
    h = _modulate(x, g_ref[...], sh_ref[...], sc_ref[...]).reshape(-1, d).astype(BF16)
    kv = _dot(h, wkv_ref[...])
    nk = k_ref.shape[-1]
    k = _group_rms(kv[:, :nk], ones_ref[...], knw_ref[...])
    v = kv[:, nk:]
    k_ref[...] = k.reshape(k_ref.shape)
    v_ref[...] = v.reshape(v_ref.shape)
    kb_ref[...] = k.astype(BF16).reshape(kb_ref.shape)
    vb_ref[...] = v.astype(BF16).reshape(vb_ref.shape)


def _kv_layer(x, tok, mods, norm_g, wkv, ones_blk, knw):
    (m_arr, sh), (_, sc) = tok.mods(mods, 0, 2)
    nk = knw.shape[-1]
    nv = wkv.shape[1] - nk
    widths = (nk, nv, nk, nv)
    dtypes = (F32, F32, BF16, BF16)
    return pl.pallas_call(
        _kv_kernel,
        grid=tok.grid,
        in_specs=[tok.spec(), _const_spec((1, tok.d)), sh, sc, _const_spec(wkv.shape),
                  _const_spec(ones_blk.shape), _const_spec(knw.shape)],
        out_specs=[tok.spec(width=w) for w in widths],
        out_shape=[tok.shape(dtype=t, width=w) for w, t in zip(widths, dtypes)],
        compiler_params=_params(len(tok.grid)),
        name="shared_kv",
    )(x, norm_g, m_arr, m_arr, wkv, ones_blk, knw)


def _diff_lambda(lp_ref, lam_init):
    lp = lp_ref[...]
    return (jnp.exp(jnp.sum(lp[0:1] * lp[1:2], axis=-1, keepdims=True))
            - jnp.exp(jnp.sum(lp[2:3] * lp[3:4], axis=-1, keepdims=True)) + lam_init)


def _query(x, g, sh, sc, wq, ones_blk, qnw):
    h = _modulate(x, g, sh, sc).reshape(-1, x.shape[-1]).astype(BF16)
    q = _group_rms(_dot(h, wq), ones_blk, qnw)
    return q * ATTN_SCALE


def _head_out(o, subw, lam_init):
    ms = jnp.mean(o * o, axis=-1, keepdims=True)
    return o * lax.rsqrt(ms + EPS) * subw * (1.0 - lam_init)


def _attn_prompt_kernel(x_ref, g_ref, sh_ref, sc_ref, gt_ref, wq_ref, ones_ref, qnw_ref,
                        kb_ref, vb_ref, lp_ref, subw_ref, wo_ref,
                        o_ref,
                        qs_ref, m_ref, acc_ref, *, lam_init):
    tq, d = x_ref.shape
    tk = tq
    i = pl.program_id(1)
    x = x_ref[...]
    q = _query(x, g_ref[...], sh_ref[...], sc_ref[...], wq_ref[...], ones_ref[...], qnw_ref[...])
    q = q * LOG2_E
    lane = lax.broadcasted_iota(jnp.int32, (tq, HEAD_DV), 1)
    first_map = lane < HEAD_DK
    th = tq // 2
    for h in range(N_HEADS):
        qh = q[:, h * HEAD_DV:(h + 1) * HEAD_DV]
        q0 = jnp.where(first_map, qh, 0.0).astype(BF16)
        q1 = jnp.where(first_map, 0.0, qh).astype(BF16)
        qs_ref[h] = jnp.concatenate([q0[:th], q1[:th], q0[th:], q1[th:]], axis=0)
    m_ref[...] = jnp.full(m_ref.shape, NEG_BIG, F32)
    acc_ref[...] = jnp.zeros(acc_ref.shape, F32)

    def attend(k0, n_keys, r0, n_rows, first_pos):
        for h in range(N_HEADS):
            hs = slice(h * HEAD_DV, (h + 1) * HEAD_DV)
            kh = kb_ref[pl.ds(k0, n_keys), hs]
            vh = vb_ref[pl.ds(k0, n_keys), hs]
            v_ext = jnp.concatenate([vh, jnp.ones_like(vh)], axis=1)
            s = _dot_nt(qs_ref[h, r0:r0 + n_rows, :], kh)
            if first_pos is not None:
                row = lax.broadcasted_iota(jnp.int32, s.shape, 0)
                col = lax.broadcasted_iota(jnp.int32, s.shape, 1)
                s = jnp.where(col <= row % th + first_pos, s, NEG_BIG)
            m_old = m_ref[h, r0:r0 + n_rows, :]
            m_new = jnp.maximum(m_old, jnp.max(s, axis=-1, keepdims=True))
            alpha = jnp.exp2(m_old - m_new)
            p = jnp.exp2(s - jnp.concatenate([m_new] * (n_keys // LANES), axis=1))
            acc_ref[h, r0:r0 + n_rows, :] = (
                jnp.concatenate([alpha, alpha], axis=1) * acc_ref[h, r0:r0 + n_rows, :]
                + _dot(p.astype(BF16), v_ext))
            m_ref[h, r0:r0 + n_rows, :] = m_new

    def body(j, carry):
        attend(pl.multiple_of(j * tk, tk), tk, 0, 2 * tq, None)
        return carry

    lax.fori_loop(0, i, body, 0)
    kd = pl.multiple_of(i * tk, tk)
    attend(kd, th, 0, tq, 0)
    attend(kd, tk, tq, tq, th)

    lam = _diff_lambda(lp_ref, lam_init)
    heads = []
    for h in range(N_HEADS):
        hs = slice(h * HEAD_DV, (h + 1) * HEAD_DV)
        o = acc_ref[h, :, :HEAD_DV] / acc_ref[h, :, HEAD_DV:]
        o0 = jnp.concatenate([o[:th], o[2 * th:3 * th]], axis=0)
        o1 = jnp.concatenate([o[th:2 * th], o[3 * th:]], axis=0)
        heads.append(_head_out(o0 - lam * o1, subw_ref[:, hs], lam_init))
    ob = jnp.concatenate(heads, axis=1).astype(BF16)
    o_ref[...] = x + gt_ref[...] * _dot(ob, wo_ref[...])


def _attn_prompt_layer(x, tok, mods, l, j, norm_g, wq, ones_blk, qnw, kb, vb, lp, subw, wo,
                       lam_init):
    (m_arr, sh), (_, sc), (_, gt) = tok.mods(mods, l, 3)
    seq, d, tq = tok.t, tok.d, tok.tt
    kvspec = pl.BlockSpec((None, seq, d), lambda b, i: (b, 0, 0), pipeline_mode=pl.Buffered(1))
    return pl.pallas_call(
        functools.partial(_attn_prompt_kernel, lam_init=lam_init),
        grid=tok.grid,
        in_specs=[tok.spec(), _const_spec((1, d)), sh, sc, gt, _layer_spec(wq, j),
                  _const_spec(ones_blk.shape), _const_spec(qnw.shape), kvspec, kvspec,
                  _const_spec(lp.shape), _const_spec(subw.shape), _layer_spec(wo, j)],
        out_specs=tok.spec(),
        out_shape=tok.shape(),
        scratch_shapes=[pltpu.VMEM((N_HEADS, 2 * tq, HEAD_DV), BF16),
                        pltpu.VMEM((N_HEADS, 2 * tq, LANES), F32),
                        pltpu.VMEM((N_HEADS, 2 * tq, 2 * HEAD_DV), F32)],
        compiler_params=_params(2),
        name="diff_attn_prompt",
    )(x, norm_g, m_arr, m_arr, m_arr, wq, ones_blk, qnw, kb, vb, lp, subw, wo)


def _query_kernel(x_ref, g_ref, sh_ref, sc_ref, wq_ref, ones_ref, qnw_ref, q_ref):
    q = _query(x_ref[...], g_ref[...], sh_ref[...], sc_ref[...], wq_ref[...], ones_ref[...],
               qnw_ref[...])
    q_ref[...] = q.astype(BF16).reshape(q_ref.shape)


def _query_layer(x, tok, mods, l, j, norm_g, wq, ones_blk, qnw):
    (m_arr, sh), (_, sc) = tok.mods(mods, l, 2)
    return pl.pallas_call(
        _query_kernel,
        grid=tok.grid,
        in_specs=[tok.spec(), _const_spec((1, tok.d)), sh, sc, _layer_spec(wq, j),
                  _const_spec(ones_blk.shape), _const_spec(qnw.shape)],
        out_specs=tok.spec(),
        out_shape=tok.shape(dtype=BF16),
        compiler_params=_params(len(tok.grid)),
        name="diff_attn_query",
    )(x, norm_g, m_arr, m_arr, wq, ones_blk, qnw)


def _attn_paged_kernel(pt_ref, q_ref, kn_ref, vn_ref, lp_ref, *rest, n_pages, lam_init):
    k_refs = rest[:n_pages]
    v_refs = rest[n_pages:2 * n_pages]
    (x_ref, g_ref, sh_ref, sc_ref, gt_ref, w13c_ref, w2c_ref, o_ref, xo_ref,
     qx_ref, m_ref, l_ref, acc_ref, h_ref, facc_ref) = rest[2 * n_pages:]
    c = pl.program_id(1)
    n_steps = pl.num_programs(1)
    t_new, d = q_ref.shape
    page = k_refs[0].shape[1]

    rows_per_head = 2 * t_new

    n_cols = w2c_ref.shape[0]
    x = x_ref[...]

    def ffn_chunk(ci):
        h = h_ref[...]
        a = _dot(h, w13c_ref[ci])
        b = _dot(h, w13c_ref[n_cols + ci])
        return _dot((a * jax.nn.sigmoid(a) * b).astype(BF16), w2c_ref[ci])

    @pl.when(c == 0)
    def _():
        h_ref[...] = _modulate(x, g_ref[...], sh_ref[...], sc_ref[...]).reshape(-1, d).astype(BF16)
        facc_ref[...] = jnp.zeros(facc_ref.shape, F32)
        qt = jnp.concatenate([q_ref[...].astype(F32)] * (2 * N_HEADS), axis=0)
        row = lax.broadcasted_iota(jnp.int32, qt.shape, 0)
        col = lax.broadcasted_iota(jnp.int32, qt.shape, 1)
        qx_ref[...] = jnp.where(row // t_new == col // HEAD_DK, qt, 0.0).astype(BF16)
        m_ref[...] = jnp.full(m_ref.shape, NEG_BIG, F32)
        l_ref[...] = jnp.zeros(l_ref.shape, F32)
        acc_ref[...] = jnp.zeros(acc_ref.shape, F32)

    def scores_update(s):
        m_old = m_ref[...]
        m_new = jnp.maximum(m_old, jnp.max(s, axis=-1, keepdims=True))
        alpha = jnp.exp(m_old - m_new)
        p = jnp.exp(s - jnp.concatenate([m_new] * (s.shape[1] // LANES), axis=1))
        l_ref[...] = alpha * l_ref[...] + jnp.sum(p, axis=-1, keepdims=True)
        m_ref[...] = m_new
        return alpha, p.astype(BF16)

    def values_update(alpha, pb, v_of_head):
        for h in range(N_HEADS):
            rs = slice(h * rows_per_head, (h + 1) * rows_per_head)
            acc_ref[rs, :] = alpha[rs] * acc_ref[rs, :] + _dot(pb[rs], v_of_head(h))

    def cached_v(h):
        return jnp.concatenate(
            [r[pl.ds(h, page, stride=N_HEADS), :].astype(BF16) for r in v_refs], axis=0)

    kt = jnp.concatenate([r[...].astype(BF16) for r in k_refs], axis=1)
    heads_per_slab = MXU_DIM // HEAD_DV
    slab_rows = heads_per_slab * rows_per_head
    s = jnp.concatenate(
        [_dot(qx_ref[g * slab_rows:(g + 1) * slab_rows, g * MXU_DIM:(g + 1) * MXU_DIM],
              kt[g * MXU_DIM:(g + 1) * MXU_DIM, :]) for g in range(d // MXU_DIM)], axis=0)
    alpha, pb = scores_update(s)
    values_update(alpha, pb, cached_v)
    facc_ref[...] += ffn_chunk(c)

    @pl.when(c + n_steps < n_cols)
    def _():
        facc_ref[...] += ffn_chunk(c + n_steps)

    @pl.when(c == n_steps - 1)
    def _():
        xo_ref[...] = x + gt_ref[...] * facc_ref[...].reshape(x.shape)
        pad = jnp.zeros((page - t_new, d), F32)
        kb_new = jnp.concatenate([kn_ref[...], pad], axis=0).astype(BF16)
        vb_new = jnp.concatenate([vn_ref[...], pad], axis=0).astype(BF16)
        s = _dot_nt(qx_ref[...], kb_new)
        row = lax.broadcasted_iota(jnp.int32, s.shape, 0)
        col = lax.broadcasted_iota(jnp.int32, s.shape, 1)
        s = jnp.where(col <= row % t_new, s, NEG_BIG)
        alpha_new, pb_new = scores_update(s)
        values_update(alpha_new, pb_new, lambda h: vb_new[:, h * HEAD_DV:(h + 1) * HEAD_DV])
        lam = _diff_lambda(lp_ref, lam_init)
        inv_l = 1.0 / l_ref[...]
        for h in range(N_HEADS):
            r0 = h * rows_per_head
            o0 = acc_ref[r0:r0 + t_new, :] * inv_l[r0:r0 + t_new]
            o1 = acc_ref[r0 + t_new:r0 + 2 * t_new, :] * inv_l[r0 + t_new:r0 + 2 * t_new]
            o_ref[:, h * HEAD_DV:(h + 1) * HEAD_DV] = o0 - lam * o1


def _attn_paged_ffn_layer(q_tm, kn_tm, vn_tm, cache_kt, cache_vr, page_table, lp, lam_init, n_pages,
                          x, tok, mods, l, norm_g, w13c, w2c):
    t_new, nb, d = q_tm.shape
    n_chunks = page_table.shape[1] // n_pages
    n_rows = 2 * N_HEADS * t_new
    assert tok.n_tiles == nb, "one FFN row tile rides with each sample sequence"
    assert w2c.shape[1] <= 2 * n_chunks, "at most two hidden-column chunks per step"
    ffn_rows = tok.tt * (tok.nb if tok.kind == "tm" else 1)
    row_spec = pl.BlockSpec((t_new, d), lambda b, c, pt: (0, b))

    def page_spec(p, shape):
        return pl.BlockSpec((None,) + shape, lambda b, c, pt: (pt[b, c * n_pages + p], 0, 0))

    k_pages = [page_spec(p, cache_kt.shape[1:]) for p in range(n_pages)]
    v_pages = [page_spec(p, cache_vr.shape[1:]) for p in range(n_pages)]
    (m_arr, sh), (_, sc), (_, gt) = tok.mods_flat(mods, l, 3)
    grid_spec = pltpu.PrefetchScalarGridSpec(
        num_scalar_prefetch=1,
        grid=(nb, n_chunks),
        in_specs=[row_spec, row_spec, row_spec,
                  pl.BlockSpec(lp.shape, lambda b, c, pt: (0, 0))] + k_pages + v_pages
        + [tok.spec_flat(), _const_spec((1, d)), sh, sc, gt, _layer_spec(w13c, l), _layer_spec(w2c, l)],
        out_specs=[row_spec, tok.spec_flat()],
        scratch_shapes=[pltpu.VMEM((n_rows, d), BF16), pltpu.VMEM((n_rows, LANES), F32),
                        pltpu.VMEM((n_rows, LANES), F32), pltpu.VMEM((n_rows, HEAD_DV), F32),
                        pltpu.VMEM((ffn_rows, d), BF16), pltpu.VMEM((ffn_rows, d), F32)],
    )
    flat = lambda a: a.reshape(t_new, nb * d)
    o, x_new = pl.pallas_call(
        functools.partial(_attn_paged_kernel, n_pages=n_pages, lam_init=lam_init),
        grid_spec=grid_spec,
        out_shape=[jax.ShapeDtypeStruct((t_new, nb * d), F32), tok.shape()],
        compiler_params=_params(2),
        name="diff_attn_paged_ffn",
    )(page_table, flat(q_tm), flat(kn_tm), flat(vn_tm), lp,
      *([cache_kt] * n_pages), *([cache_vr] * n_pages),
      x, norm_g, m_arr, m_arr, m_arr, w13c, w2c)
    return o.reshape(t_new, nb, d), x_new


def _attn_out_kernel(x_ref, o_ref_in, gt_ref, subw_ref, wo_ref, out_ref, *, lam_init):
    x = x_ref[...]
    d = x.shape[-1]
    o = o_ref_in[...].reshape(-1, d)
    heads = []
    for h in range(N_HEADS):
        hs = slice(h * HEAD_DV, (h + 1) * HEAD_DV)
        heads.append(_head_out(o[:, hs], subw_ref[:, hs], lam_init))
    ob = jnp.concatenate(heads, axis=1).astype(BF16)
    out_ref[...] = x + gt_ref[...] * _dot(ob, wo_ref[...]).reshape(x.shape)


def _attn_out_layer(x, o, tok, mods, l, j, subw, wo, lam_init):
    m_arr, gt = tok.mods(mods, l, 3)[2]
    return pl.pallas_call(
        functools.partial(_attn_out_kernel, lam_init=lam_init),
        grid=tok.grid,
        in_specs=[tok.spec(), tok.spec(), gt, _const_spec(subw.shape), _layer_spec(wo, j)],
        out_specs=tok.spec(),
        out_shape=tok.shape(),
        compiler_params=_params(len(tok.grid)),
        name="diff_attn_out",
    )(x, o, m_arr, subw, wo)


def kernel(x_prompt, x_sample, state_ssm_re, state_ssm_im, cache_k, cache_v, page_table, c_prompt, c_sample, norm_mix, w_ada_mix, b_ada_mix, norm_ffn, w_ada_ffn, b_ada_ffn, ffn_w13, ffn_w2, ssm_lam_re, ssm_lam_im, ssm_log_step, ssm_b_re, ssm_b_im, ssm_c_re, ssm_c_im, ssm_d, ssm_w_glu, norm_kv, w_ada_kv, b_ada_kv, w_kv, k_norm, w_q, q_norm, diff_lambda, subln, w_o):
    pb, seq, d = x_prompt.shape
    db, dseq, _ = x_sample.shape
    depth = norm_mix.shape[0]
    n_a, g, n = ssm_lam_re.shape
    gn = g * n
    nk = N_HEADS * 2 * HEAD_DK

    c_all = jnp.concatenate([c_sample, c_prompt], axis=0)
    mods_mix = _Mods(_ada_mods(c_all, w_ada_mix, b_ada_mix), d, db)
    mods_ffn = _Mods(_ada_mods(c_all, w_ada_ffn, b_ada_ffn), d, db)
    mods_kv = _Mods(_ada_mods(c_all, w_ada_kv[None], b_ada_kv[None]), d, db)

    a_re, a_im, wb = _discretize(ssm_lam_re, ssm_lam_im, ssm_log_step, ssm_b_re, ssm_b_im)
    wc = _block_diag_out(ssm_c_re, ssm_c_im)
    wglu, wkv, wq, wo = _to_bf16(ssm_w_glu), _to_bf16(w_kv), _to_bf16(w_q), _to_bf16(w_o)
    w13c = _to_bf16_col_chunks(ffn_w13, MXU_DIM)
    w2c = _to_bf16(ffn_w2).reshape(depth, -1, MXU_DIM, d)
    row = lambda a: a.reshape(1, -1)
    knw = row(jnp.tile(k_norm.reshape(-1), N_HEADS))
    qnw = [row(jnp.tile(q_norm[j].reshape(-1), N_HEADS)) for j in range(depth - n_a)]
    subw = [row(jnp.tile(subln[j], N_HEADS)) for j in range(depth - n_a)]
    blk = jnp.arange(MXU_DIM) // HEAD_DK
    ones_blk = ((blk[:, None] == blk[None, :]).astype(F32) / HEAD_DK).astype(BF16)
    lam_inits = [0.8 - 0.6 * math.exp(-0.3 * layer) for layer in range(n_a, depth)]

    n_pool, page = cache_k.shape[:2]
    ck = cache_k.transpose(0, 2, 3, 4, 1).reshape(n_pool, nk, page)
    cv = cache_v.reshape(n_pool, page * N_HEADS, HEAD_DV)

    n_b = depth - n_a
    assert n_b <= n_a, "every paged attention layer needs an S5-phase FFN of the prompt group to host"

    def s5(x, a, tc, bm_in, bm_out, row_block, h0_re, h0_im):
        return _s5_layer(x, tc, bm_in, bm_out, mods_mix, a, row_block, row(norm_mix[a]),
                         h0_re[a], h0_im[a], a_re[a], a_im[a], wb, wc, row(ssm_d[a]), wglu)

    def ffn(x, tok, layer):
        return _ffn_layer(x, tok, mods_ffn, layer, row(norm_ffn[layer]), w13c, w2c)

    def states(parts, nb):
        return jnp.stack([p.reshape(nb, g, n) for p in parts])

    tok_s = _Tokens("tm", db, dseq, d, db * dseq)
    xs = x_sample.transpose(1, 0, 2)
    hs_re, hs_im = state_ssm_re.reshape(n_a, db, gn).astype(F32), state_ssm_im.reshape(n_a, db, gn).astype(F32)
    re_s, im_s = [], []
    for a in range(n_a):
        xs, hr, hi = s5(xs, a, dseq, False, False, 0, hs_re, hs_im)
        re_s.append(hr)
        im_s.append(hi)
        xs = ffn(xs, tok_s, a)
    k_s, v_s, _, _ = _kv_layer(xs, tok_s, mods_kv, row(norm_kv), wkv, ones_blk, knw)

    tok_tm = _Tokens("tm", pb, seq, d, PROMPT_ROWS, mod_row_block=db // pb)
    tok_bm = _Tokens("bm", pb, seq, d, PROMPT_ROWS)
    zeros = jnp.zeros((n_a, pb, gn), F32)
    xp = x_prompt
    re_p, im_p = [], []
    for a in range(n_a):
        last = a == n_a - 1
        xp, hr, hi = s5(xp, a, PROMPT_ROWS // pb, a == 0, last, db // pb, zeros, zeros)
        re_p.append(hr)
        im_p.append(hi)
        tok = tok_bm if last else tok_tm
        if a < n_b:
            j, layer = a, n_a + a
            q = _query_layer(xs, tok_s, mods_mix, layer, j, row(norm_mix[layer]), wq, ones_blk, qnw[j])
            o, xp = _attn_paged_ffn_layer(q, k_s, v_s, ck, cv, page_table, diff_lambda[j], lam_inits[j],
                                          PAGES_PER_STEP, xp, tok, mods_ffn, a, row(norm_ffn[a]),
                                          w13c, w2c)
            xs = _attn_out_layer(xs, o, tok_s, mods_mix, layer, j, subw[j], wo, lam_inits[j])
            xs = ffn(xs, tok_s, layer)
        else:
            xp = ffn(xp, tok, a)

    k_p, v_p, kb, vb = _kv_layer(xp, tok_bm, mods_kv, row(norm_kv), wkv, ones_blk, knw)
    for j in range(n_b):
        layer = n_a + j
        xp = _attn_prompt_layer(xp, tok_bm, mods_mix, layer, j, row(norm_mix[layer]), wq, ones_blk,
                                qnw[j], kb, vb, diff_lambda[j], subw[j], wo, lam_inits[j])
        xp = ffn(xp, tok_bm, layer)

    y_p, y_s = xp, xs
    re_p, im_p, re_s, im_s = states(re_p, pb), states(im_p, pb), states(re_s, db), states(im_s, db)
    tm2bm = lambda a: a.transpose(1, 0, 2)

    return (y_p, tm2bm(y_s), re_p, im_p, re_s, im_s,
            k_p.reshape(pb, seq, N_HEADS, 2, HEAD_DK), v_p.reshape(pb, seq, N_HEADS, HEAD_DV),
            tm2bm(k_s).reshape(db, dseq, N_HEADS, 2, HEAD_DK), tm2bm(v_s).reshape(db, dseq, N_HEADS, HEAD_DV))
```

```python
import functools
import math

import jax
import jax.numpy as jnp
from jax import lax
from jax.experimental import pallas as pl
from jax.experimental.pallas import tpu as pltpu

F32 = jnp.float32
BF16 = jnp.bfloat16

EPS = 1e-6
N_HEADS = 8
HEAD_DK = 64
HEAD_DV = 2 * HEAD_DK
ATTN_SCALE = HEAD_DK ** -0.5
SSM_GROUP = 16
LANES = 128
MXU_DIM = 256
GROUPS_PER_BLOCK = MXU_DIM // SSM_GROUP
VMEM_LIMIT = 56 * 1024 * 1024
NEG_BIG = -1e30
PROMPT_ROWS = 512
PAGES_PER_STEP = 8
LOG2_E = math.log2(math.e)


def _params(n_grid, vmem=VMEM_LIMIT):
    return pltpu.CompilerParams(dimension_semantics=("arbitrary",) * n_grid,
                                vmem_limit_bytes=vmem)


def _const_spec(shape):
    nd = len(shape)
    return pl.BlockSpec(tuple(shape), lambda *_: (0,) * nd, pipeline_mode=pl.Buffered(1))


def _layer_spec(stacked, l):
    nd = stacked.ndim - 1
    return pl.BlockSpec((None,) + stacked.shape[1:], lambda *_: (l,) + (0,) * nd,
                        pipeline_mode=pl.Buffered(1))


def _dot(a, b):
    return jnp.dot(a, b, preferred_element_type=F32)


def _dot_nt(a, b):
    return lax.dot_general(a, b, (((1,), (1,)), ((), ())), preferred_element_type=F32)


def _modulate(x, g, shift, scale):
    ms = jnp.mean(x * x, axis=-1, keepdims=True)
    xn = x * lax.rsqrt(ms + EPS) * g
    return xn * (1.0 + scale) + shift


def _group_rms(y, ones_blk, w):
    parts = []
    for j in range(y.shape[1] // MXU_DIM):
        blk = y[:, j * MXU_DIM:(j + 1) * MXU_DIM]
        parts.append(_dot((blk * blk).astype(BF16), ones_blk))
    ms = jnp.concatenate(parts, axis=1)
    return y * lax.rsqrt(ms + EPS) * w


CAST_BLOCK_BYTES = 6 * 1024 * 1024


def _cast_kernel(w_ref, o_ref):
    o_ref[...] = w_ref[...].astype(o_ref.dtype)


def _to_bf16(w):
    cols = w.shape[-1]
    w2 = w.reshape(-1, cols)
    n_rows = w2.shape[0]
    rb = n_rows
    while rb * cols * 4 > CAST_BLOCK_BYTES and rb % 32 == 0:
        rb //= 2
    out = pl.pallas_call(
        _cast_kernel,
        grid=(n_rows // rb,),
        in_specs=[pl.BlockSpec((rb, cols), lambda i: (i, 0))],
        out_specs=pl.BlockSpec((rb, cols), lambda i: (i, 0)),
        out_shape=jax.ShapeDtypeStruct(w2.shape, BF16),
        compiler_params=_params(1),
        name="cast_bf16",
    )(w2)
    return out.reshape(w.shape)


def _cast_chunks_kernel(w_ref, o_ref):
    n_chunks, _, chunk = o_ref.shape
    for j in range(n_chunks):
        o_ref[j] = w_ref[:, j * chunk:(j + 1) * chunk].astype(o_ref.dtype)


def _to_bf16_col_chunks(w, chunk):
    n_layers, k, n = w.shape
    per_step = n // (2 * chunk)
    return pl.pallas_call(
        _cast_chunks_kernel,
        grid=(n_layers, 2),
        in_specs=[pl.BlockSpec((None, k, n // 2), lambda l, j: (l, 0, j))],
        out_specs=pl.BlockSpec((None, per_step, k, chunk), lambda l, j: (l, j, 0, 0)),
        out_shape=jax.ShapeDtypeStruct((n_layers, n // chunk, k, chunk), BF16),
        compiler_params=_params(2),
        name="cast_bf16_chunks",
    )(w)


def _ada_kernel(c_ref, w_ref, b_ref, o_ref):
    c = c_ref[...]
    s = (c * jax.nn.sigmoid(c)).astype(BF16)
    o_ref[...] = _dot(s, w_ref[...].astype(BF16)) + b_ref[...]


def _ada_mods(c_all, w, b):
    nl, d, n = w.shape
    r = c_all.shape[0]
    tn = 1024
    return pl.pallas_call(
        _ada_kernel,
        grid=(nl, n // tn),
        in_specs=[pl.BlockSpec((r, d), lambda l, j: (0, 0)),
                  pl.BlockSpec((None, d, tn), lambda l, j: (l, 0, j)),
                  pl.BlockSpec((None, 1, tn), lambda l, j: (l, 0, j))],
        out_specs=pl.BlockSpec((None, r, tn), lambda l, j: (l, 0, j)),
        out_shape=jax.ShapeDtypeStruct((nl, r, n), F32),
        compiler_params=_params(2),
        name="ada_mods",
    )(c_all, w, b.reshape(nl, 1, n))


class _Mods:
    def __init__(self, arr, d, db):
        self.a3 = arr
        self.a4 = arr.reshape(arr.shape[0], arr.shape[1], 1, arr.shape[2])
        self.d = d
        self.db = db

    def spec_rows(self, l, k, row_block, n_rows):
        return self.a3, pl.BlockSpec((None, n_rows, self.d), lambda *_: (l, row_block, k))

    def spec_prompt_row(self, l, k, tiles_per_seq=1):
        db = self.db
        return self.a4, pl.BlockSpec((None, None, 1, self.d),
                                     lambda b, *_: (l, db + b // tiles_per_seq, 0, k))


def _disc_kernel(lr_ref, li_ref, ls_ref, br_ref, bi_ref, ar_ref, ai_ref, wb_ref, *, n_state):
    dt = jnp.exp(ls_ref[...])
    lr, li = lr_ref[...], li_ref[...]
    mag = jnp.exp(lr * dt)
    a_re, a_im = mag * jnp.cos(li * dt), mag * jnp.sin(li * dt)
    nr, ni = a_re - 1.0, a_im
    den = lr * lr + li * li
    f_re = (nr * lr + ni * li) / den
    f_im = (ni * lr - nr * li) / den
    br, bi = br_ref[...], bi_ref[...]
    ar_ref[...] = a_re
    ai_ref[...] = a_im
    bb_re = f_re * br - f_im * bi
    bb_im = f_re * bi + f_im * br
    nj, kin, two_half = wb_ref.shape
    half = two_half // 2
    p = br.shape[0]
    row = lax.broadcasted_iota(jnp.int32, (kin, half), 0)
    col = lax.broadcasted_iota(jnp.int32, (kin, half), 1)
    own = row // p == col // n_state
    for j in range(nj):
        cols = slice(j * half, (j + 1) * half)
        for part, bb in enumerate((bb_re, bb_im)):
            tiled = jnp.concatenate([bb[:, cols]] * (kin // p), axis=0)
            wb_ref[j, :, part * half:(part + 1) * half] = jnp.where(own, tiled, 0.0).astype(BF16)


def _discretize(lam_re, lam_im, log_step, b_re, b_im):
    na, g, n = lam_re.shape
    p = b_re.shape[-1]
    gn = g * n
    gb = GROUPS_PER_BLOCK
    nj = g // gb
    vec = lambda a: a.reshape(na, 1, gn)
    ls = jnp.broadcast_to(log_step[:, :, None], (na, g, n)).reshape(na, 1, gn)
    bt = lambda a: a.transpose(0, 3, 1, 2).reshape(na, p, gn)
    row = pl.BlockSpec((None, 1, gn), lambda a: (a, 0, 0))
    mat = pl.BlockSpec((None, p, gn), lambda a: (a, 0, 0))
    wb_shape = (na, nj, gb * p, 2 * gb * n)
    return pl.pallas_call(
        functools.partial(_disc_kernel, n_state=n),
        grid=(na,),
        in_specs=[row, row, row, mat, mat],
        out_specs=[row, row, pl.BlockSpec((None,) + wb_shape[1:], lambda a: (a, 0, 0, 0))],
        out_shape=[jax.ShapeDtypeStruct((na, 1, gn), F32)] * 2
        + [jax.ShapeDtypeStruct(wb_shape, BF16)],
        compiler_params=_params(1),
        name="s5_discretize",
    )(vec(lam_re), vec(lam_im), ls, bt(b_re), bt(b_im))


def _block_diag_out(c_re, c_im):
    na, g, p, n = c_re.shape
    gb = GROUPS_PER_BLOCK
    nj = g // gb
    eye = jnp.eye(gb, dtype=F32)

    def one(c):
        c5 = c.reshape(na, nj, gb, p, n)
        w = jnp.einsum('ajgpn,gh->ajgnhp', c5, eye)
        return w.reshape(na, nj, gb * n, gb * p)

    return jnp.concatenate([one(c_re), one(-c_im)], axis=2).astype(BF16)


def _s5_kernel(x_ref, g_ref, sh_ref, sc_ref, gt_ref, h0r_ref, h0i_ref, ar_ref, ai_ref,
               wb_ref, wc_ref, dsk_ref, wglu_ref,
               xo_ref, hfr_ref, hfi_ref,
               str_ref, sti_ref, bu_ref, y_ref, xt_ref, *, col_w, bm_in, bm_out):
    if bm_in:
        nb, tc, d = x_ref.shape
    else:
        tc, nb, d = x_ref.shape
    rows = tc * nb
    nj = wb_ref.shape[0]
    half = wb_ref.shape[2] // 2

    @pl.when(pl.program_id(0) == 0)
    def _():
        str_ref[...] = h0r_ref[...]
        sti_ref[...] = h0i_ref[...]

    lane_blocks = [slice(c * LANES, (c + 1) * LANES) for c in range(d // LANES)]
    if bm_in:
        for b in range(nb):
            for c, cols in enumerate(lane_blocks):
                xt_ref[c, pl.ds(b, tc, stride=nb), :] = x_ref[b, :, cols]
        x = jnp.concatenate([xt_ref[c] for c in range(len(lane_blocks))], axis=1)
        x = x.reshape(tc, nb, d)
    else:
        x = x_ref[...]
    u = _modulate(x, g_ref[...], sh_ref[...], sc_ref[...]).reshape(rows, d)
    ub = u.astype(BF16)

    for j in range(nj):
        kin = wb_ref.shape[1]
        bu_ref[...] = _dot(ub[:, j * kin:(j + 1) * kin], wb_ref[j])
        for cs in range(0, half, col_w):
            g0 = j * half + cs
            ar = jnp.broadcast_to(ar_ref[:, g0:g0 + col_w], (nb, col_w))
            ai = jnp.broadcast_to(ai_ref[:, g0:g0 + col_w], (nb, col_w))

            def step(t, carry, cs=cs, ar=ar, ai=ai):
                hr, hi = carry
                r0 = pl.multiple_of(t * nb, nb)
                bur = bu_ref[pl.ds(r0, nb), cs:cs + col_w]
                bui = bu_ref[pl.ds(r0, nb), half + cs:half + cs + col_w]
                nhr = ar * hr - ai * hi + bur
                nhi = ar * hi + ai * hr + bui
                bu_ref[pl.ds(r0, nb), cs:cs + col_w] = nhr
                bu_ref[pl.ds(r0, nb), half + cs:half + cs + col_w] = nhi
                return nhr, nhi

            hr, hi = lax.fori_loop(0, tc, step,
                                   (str_ref[:, g0:g0 + col_w], sti_ref[:, g0:g0 + col_w]),
                                   unroll=True)
            str_ref[:, g0:g0 + col_w] = hr
            sti_ref[:, g0:g0 + col_w] = hi
        kout = wc_ref.shape[2]
        y_ref[:, j * kout:(j + 1) * kout] = _dot(bu_ref[...].astype(BF16), wc_ref[j])

    y = y_ref[...] + dsk_ref[...] * u
    z = _dot(jax.nn.gelu(y).astype(BF16), wglu_ref[...])
    out = z[:, :d] * jax.nn.sigmoid(z[:, d:])
    res = x + gt_ref[...] * out.reshape(tc, nb, d)
    if bm_out:
        res = res.reshape(rows, d)
        for c, cols in enumerate(lane_blocks):
            xt_ref[c] = res[:, cols]
        for b in range(nb):
            for c, cols in enumerate(lane_blocks):
                xo_ref[b, :, cols] = xt_ref[c, pl.ds(b, tc, stride=nb), :]
    else:
        xo_ref[...] = res
    hfr_ref[...] = str_ref[...]
    hfi_ref[...] = sti_ref[...]


def _s5_layer(x, tc, bm_in, bm_out, mods, l, row_block, norm_g, h0_re, h0_im, a_re, a_im,
              wb, wc, dskip, wglu):
    if bm_in:
        nb, t, d = x.shape
    else:
        t, nb, d = x.shape
    gn = a_re.shape[-1]
    rows = tc * nb
    col_w = max(LANES, (8 * 512) // nb)
    tm_spec = pl.BlockSpec((tc, nb, d), lambda c: (c, 0, 0))
    bm_spec = pl.BlockSpec((nb, tc, d), lambda c: (0, c, 0))
    (m_arr, sh), (_, sc), (_, gt) = (mods.spec_rows(l, k, row_block, nb) for k in range(3))
    st_spec = pl.BlockSpec((nb, gn), lambda c: (0, 0))
    return pl.pallas_call(
        functools.partial(_s5_kernel, col_w=col_w, bm_in=bm_in, bm_out=bm_out),
        grid=(t // tc,),
        in_specs=[bm_spec if bm_in else tm_spec, _const_spec((1, d)), sh, sc, gt, st_spec, st_spec,
                  _const_spec((1, gn)), _const_spec((1, gn)),
                  _layer_spec(wb, l), _layer_spec(wc, l),
                  _const_spec((1, d)), _layer_spec(wglu, l)],
        out_specs=[bm_spec if bm_out else tm_spec, st_spec, st_spec],
        out_shape=[jax.ShapeDtypeStruct((nb, t, d) if bm_out else (t, nb, d), F32),
                   jax.ShapeDtypeStruct((nb, gn), F32), jax.ShapeDtypeStruct((nb, gn), F32)],
        scratch_shapes=[pltpu.VMEM((nb, gn), F32), pltpu.VMEM((nb, gn), F32),
                        pltpu.VMEM((rows, wb.shape[3]), F32), pltpu.VMEM((rows, d), F32),
                        pltpu.VMEM((d // LANES, rows, LANES) if (bm_in or bm_out)
                                   else (1, 8, LANES), F32)],
        compiler_params=_params(1),
        name="s5_layer",
    )(x, norm_g, m_arr, m_arr, m_arr, h0_re, h0_im, a_re, a_im, wb, wc, dskip, wglu)


class _Tokens:
    def __init__(self, kind, nb, t, d, rows, mod_row_block=0):
        self.kind, self.nb, self.t, self.d = kind, nb, t, d
        self.tt = rows // nb if kind == "tm" else rows
        self.mod_row_block = mod_row_block

    @property
    def grid(self):
        return (self.t // self.tt,) if self.kind == "tm" else (self.nb, self.t // self.tt)

    def spec(self, width=None):
        w = self.d if width is None else width
        if self.kind == "tm":
            return pl.BlockSpec((self.tt, self.nb, w), lambda i: (i, 0, 0))
        return pl.BlockSpec((None, self.tt, w), lambda b, i: (b, i, 0))

    def shape(self, dtype=F32, width=None):
        w = self.d if width is None else width
        dims = (self.t, self.nb, w) if self.kind == "tm" else (self.nb, self.t, w)
        return jax.ShapeDtypeStruct(dims, dtype)

    def mods(self, mods, l, n):
        if self.kind == "tm":
            return [mods.spec_rows(l, k, self.mod_row_block, self.nb) for k in range(n)]
        return [mods.spec_prompt_row(l, k) for k in range(n)]

    @property
    def n_tiles(self):
        return math.prod(self.grid)

    def spec_flat(self):
        if self.kind == "tm":
            return pl.BlockSpec((self.tt, self.nb, self.d), lambda s, *_: (s, 0, 0))
        per_seq = self.t // self.tt
        return pl.BlockSpec((None, self.tt, self.d), lambda s, *_: (s // per_seq, s % per_seq, 0))

    def mods_flat(self, mods, l, n):
        if self.kind == "tm":
            return self.mods(mods, l, n)
        return [mods.spec_prompt_row(l, k, tiles_per_seq=self.t // self.tt) for k in range(n)]


FFN_GROUP = 4


def _ffn_kernel(x_ref, g_ref, sh_ref, sc_ref, gt_ref, w13c_ref, w2c_ref, o_ref):
    x = x_ref[...]
    d = x.shape[-1]
    n_cols, cw, _ = w2c_ref.shape
    h = _modulate(x, g_ref[...], sh_ref[...], sc_ref[...]).reshape(-1, d).astype(BF16)
    acc = None
    for c0 in range(0, n_cols, FFN_GROUP):
        cs = range(c0, min(n_cols, c0 + FFN_GROUP))
        a = jnp.concatenate([_dot(h, w13c_ref[ci]) for ci in cs], axis=1)
        b = jnp.concatenate([_dot(h, w13c_ref[n_cols + ci]) for ci in cs], axis=1)
        gte = (a * jax.nn.sigmoid(a) * b).astype(BF16)
        part = _dot(gte, w2c_ref[c0:c0 + len(cs)].reshape(len(cs) * cw, d))
        acc = part if acc is None else acc + part
    o_ref[...] = x + gt_ref[...] * acc.reshape(x.shape)


def _ffn_layer(x, tok, mods, l, norm_g, w13c, w2c):
    (m_arr, sh), (_, sc), (_, gt) = tok.mods(mods, l, 3)
    return pl.pallas_call(
        _ffn_kernel,
        grid=tok.grid,
        in_specs=[tok.spec(), _const_spec((1, tok.d)), sh, sc, gt,
                  _layer_spec(w13c, l), _layer_spec(w2c, l)],
        out_specs=tok.spec(),
        out_shape=tok.shape(),
        compiler_params=_params(len(tok.grid)),
        name="swiglu_ffn",
    )(x, norm_g, m_arr, m_arr, m_arr, w13c, w2c)


def _kv_kernel(x_ref, g_ref, sh_ref, sc_ref, wkv_ref, ones_ref, knw_ref,
               k_ref, v_ref, kb_ref, vb_ref):
    x = x_ref[...]
    d = x.shape[-1]
    h = _modulate(x, g_ref[...], sh_ref[...], sc_ref[...]).reshape(-1, d).astype(BF16)
    kv = _dot(h, wkv_ref[...])
    nk = k_ref.shape[-1]
    k = _group_rms(kv[:, :nk], ones_ref[...], knw_ref[...])
    v = kv[:, nk:]
    k_ref[...] = k.reshape(k_ref.shape)
    v_ref[...] = v.reshape(v_ref.shape)
    kb_ref[...] = k.astype(BF16).reshape(kb_ref.shape)
    vb_ref[...] = v.astype(BF16).reshape(vb_ref.shape)


def _kv_layer(x, tok, mods, norm_g, wkv, ones_blk, knw):
    (m_arr, sh), (_, sc) = tok.mods(mods, 0, 2)
    nk = knw.shape[-1]
    nv = wkv.shape[1] - nk
    widths = (nk, nv, nk, nv)
    dtypes = (F32, F32, BF16, BF16)
    return pl.pallas_call(
        _kv_kernel,
        grid=tok.grid,
        in_specs=[tok.spec(), _const_spec((1, tok.d)), sh, sc, _const_spec(wkv.shape),
                  _const_spec(ones_blk.shape), _const_spec(knw.shape)],
        out_specs=[tok.spec(width=w) for w in widths],
        out_shape=[tok.shape(dtype=t, width=w) for w, t in zip(widths, dtypes)],
        compiler_params=_params(len(tok.grid)),
        name="shared_kv",
    )(x, norm_g, m_arr, m_arr, wkv, ones_blk, knw)


def _diff_lambda(lp_ref, lam_init):
    lp = lp_ref[...]
    return (jnp.exp(jnp.sum(lp[0:1] * lp[1:2], axis=-1, keepdims=True))
            - jnp.exp(jnp.sum(lp[2:3] * lp[3:4], axis=-1, keepdims=True)) + lam_init)


def _query(x, g, sh, sc, wq, ones_blk, qnw):
    h = _modulate(x, g, sh, sc).reshape(-1, x.shape[-1]).astype(BF16)
    q = _group_rms(_dot(h, wq), ones_blk, qnw)
    return q * ATTN_SCALE


def _head_out(o, subw, lam_init):
    ms = jnp.mean(o * o, axis=-1, keepdims=True)
    return o * lax.rsqrt(ms + EPS) * subw * (1.0 - lam_init)


def _attn_prompt_kernel(x_ref, g_ref, sh_ref, sc_ref, gt_ref, wq_ref, ones_ref, qnw_ref,
                        kb_ref, vb_ref, lp_ref, subw_ref, wo_ref,
                        o_ref,
                        qs_ref, m_ref, acc_ref, *, lam_init):
    tq, d = x_ref.shape
    tk = tq
    i = pl.program_id(1)
    x = x_ref[...]
    q = _query(x, g_ref[...], sh_ref[...], sc_ref[...], wq_ref[...], ones_ref[...], qnw_ref[...])
    q = q * LOG2_E
    lane = lax.broadcasted_iota(jnp.int32, (tq, HEAD_DV), 1)
    first_map = lane < HEAD_DK
    th = tq // 2
    for h in range(N_HEADS):
        qh = q[:, h * HEAD_DV:(h + 1) * HEAD_DV]
        q0 = jnp.where(first_map, qh, 0.0).astype(BF16)
        q1 = jnp.where(first_map, 0.0, qh).astype(BF16)
        qs_ref[h] = jnp.concatenate([q0[:th], q1[:th], q0[th:], q1[th:]], axis=0)
    m_ref[...] = jnp.full(m_ref.shape, NEG_BIG, F32)
    acc_ref[...] = jnp.zeros(acc_ref.shape, F32)

    def attend(k0, n_keys, r0, n_rows, first_pos):
        for h in range(N_HEADS):
            hs = slice(h * HEAD_DV, (h + 1) * HEAD_DV)
            kh = kb_ref[pl.ds(k0, n_keys), hs]
            vh = vb_ref[pl.ds(k0, n_keys), hs]
            v_ext = jnp.concatenate([vh, jnp.ones_like(vh)], axis=1)
            s = _dot_nt(qs_ref[h, r0:r0 + n_rows, :], kh)
            if first_pos is not None:
                row = lax.broadcasted_iota(jnp.int32, s.shape, 0)
                col = lax.broadcasted_iota(jnp.int32, s.shape, 1)
                s = jnp.where(col <= row % th + first_pos, s, NEG_BIG)
            m_old = m_ref[h, r0:r0 + n_rows, :]
            m_new = jnp.maximum(m_old, jnp.max(s, axis=-1, keepdims=True))
            alpha = jnp.exp2(m_old - m_new)
            p = jnp.exp2(s - jnp.concatenate([m_new] * (n_keys // LANES), axis=1))
            acc_ref[h, r0:r0 + n_rows, :] = (
                jnp.concatenate([alpha, alpha], axis=1) * acc_ref[h, r0:r0 + n_rows, :]
                + _dot(p.astype(BF16), v_ext))
            m_ref[h, r0:r0 + n_rows, :] = m_new

    def body(j, carry):
        attend(pl.multiple_of(j * tk, tk), tk, 0, 2 * tq, None)
        return carry

    lax.fori_loop(0, i, body, 0)
    kd = pl.multiple_of(i * tk, tk)
    attend(kd, th, 0, tq, 0)
    attend(kd, tk, tq, tq, th)

    lam = _diff_lambda(lp_ref, lam_init)
    heads = []
    for h in range(N_HEADS):
        hs = slice(h * HEAD_DV, (h + 1) * HEAD_DV)
        o = acc_ref[h, :, :HEAD_DV] / acc_ref[h, :, HEAD_DV:]
        o0 = jnp.concatenate([o[:th], o[2 * th:3 * th]], axis=0)
        o1 = jnp.concatenate([o[th:2 * th], o[3 * th:]], axis=0)
        heads.append(_head_out(o0 - lam * o1, subw_ref[:, hs], lam_init))
    ob = jnp.concatenate(heads, axis=1).astype(BF16)
    o_ref[...] = x + gt_ref[...] * _dot(ob, wo_ref[...])


def _attn_prompt_layer(x, tok, mods, l, j, norm_g, wq, ones_blk, qnw, kb, vb, lp, subw, wo,
                       lam_init):
    (m_arr, sh), (_, sc), (_, gt) = tok.mods(mods, l, 3)
    seq, d, tq = tok.t, tok.d, tok.tt
    kvspec = pl.BlockSpec((None, seq, d), lambda b, i: (b, 0, 0), pipeline_mode=pl.Buffered(1))
    return pl.pallas_call(
        functools.partial(_attn_prompt_kernel, lam_init=lam_init),
        grid=tok.grid,
        in_specs=[tok.spec(), _const_spec((1, d)), sh, sc, gt, _layer_spec(wq, j),
                  _const_spec(ones_blk.shape), _const_spec(qnw.shape), kvspec, kvspec,
                  _const_spec(lp.shape), _const_spec(subw.shape), _layer_spec(wo, j)],
        out_specs=tok.spec(),
        out_shape=tok.shape(),
        scratch_shapes=[pltpu.VMEM((N_HEADS, 2 * tq, HEAD_DV), BF16),
                        pltpu.VMEM((N_HEADS, 2 * tq, LANES), F32),
                        pltpu.VMEM((N_HEADS, 2 * tq, 2 * HEAD_DV), F32)],
        compiler_params=_params(2),
        name="diff_attn_prompt",
    )(x, norm_g, m_arr, m_arr, m_arr, wq, ones_blk, qnw, kb, vb, lp, subw, wo)


def _query_kernel(x_ref, g_ref, sh_ref, sc_ref, wq_ref, ones_ref, qnw_ref, q_ref):
    q = _query(x_ref[...], g_ref[...], sh_ref[...], sc_ref[...], wq_ref[...], ones_ref[...],
               qnw_ref[...])
    q_ref[...] = q.astype(BF16).reshape(q_ref.shape)


def _query_layer(x, tok, mods, l, j, norm_g, wq, ones_blk, qnw):
    (m_arr, sh), (_, sc) = tok.mods(mods, l, 2)
    return pl.pallas_call(
        _query_kernel,
        grid=tok.grid,
        in_specs=[tok.spec(), _const_spec((1, tok.d)), sh, sc, _layer_spec(wq, j),
                  _const_spec(ones_blk.shape), _const_spec(qnw.shape)],
        out_specs=tok.spec(),
        out_shape=tok.shape(dtype=BF16),
        compiler_params=_params(len(tok.grid)),
        name="diff_attn_query",
    )(x, norm_g, m_arr, m_arr, wq, ones_blk, qnw)


def _attn_paged_kernel(pt_ref, q_ref, kn_ref, vn_ref, lp_ref, *rest, n_pages, lam_init):
    k_refs = rest[:n_pages]
    v_refs = rest[n_pages:2 * n_pages]
    (x_ref, g_ref, sh_ref, sc_ref, gt_ref, w13c_ref, w2c_ref, o_ref, xo_ref,
     qx_ref, m_ref, l_ref, acc_ref, h_ref, facc_ref) = rest[2 * n_pages:]
    c = pl.program_id(1)
    n_steps = pl.num_programs(1)
    t_new, d = q_ref.shape
    page = k_refs[0].shape[1]

    rows_per_head = 2 * t_new

    n_cols = w2c_ref.shape[0]
    x = x_ref[...]

    def ffn_chunk(ci):
        h = h_ref[...]
        a = _dot(h, w13c_ref[ci])
        b = _dot(h, w13c_ref[n_cols + ci])
        return _dot((a * jax.nn.sigmoid(a) * b).astype(BF16), w2c_ref[ci])

    @pl.when(c == 0)
    def _():
        h_ref[...] = _modulate(x, g_ref[...], sh_ref[...], sc_ref[...]).reshape(-1, d).astype(BF16)
        facc_ref[...] = jnp.zeros(facc_ref.shape, F32)
        qt = jnp.concatenate([q_ref[...].astype(F32)] * (2 * N_HEADS), axis=0)
        row = lax.broadcasted_iota(jnp.int32, qt.shape, 0)
        col = lax.broadcasted_iota(jnp.int32, qt.shape, 1)
        qx_ref[...] = jnp.where(row // t_new == col // HEAD_DK, qt, 0.0).astype(BF16)
        m_ref[...] = jnp.full(m_ref.shape, NEG_BIG, F32)
        l_ref[...] = jnp.zeros(l_ref.shape, F32)
        acc_ref[...] = jnp.zeros(acc_ref.shape, F32)

    def scores_update(s):
        m_old = m_ref[...]
        m_new = jnp.maximum(m_old, jnp.max(s, axis=-1, keepdims=True))
        alpha = jnp.exp(m_old - m_new)
        p = jnp.exp(s - jnp.concatenate([m_new] * (s.shape[1] // LANES), axis=1))
        l_ref[...] = alpha * l_ref[...] + jnp.sum(p, axis=-1, keepdims=True)
        m_ref[...] = m_new
        return alpha, p.astype(BF16)

    def values_update(alpha, pb, v_of_head):
        for h in range(N_HEADS):
            rs = slice(h * rows_per_head, (h + 1) * rows_per_head)
            acc_ref[rs, :] = alpha[rs] * acc_ref[rs, :] + _dot(pb[rs], v_of_head(h))

    def cached_v(h):
        return jnp.concatenate(
            [r[pl.ds(h, page, stride=N_HEADS), :].astype(BF16) for r in v_refs], axis=0)

    kt = jnp.concatenate([r[...].astype(BF16) for r in k_refs], axis=1)
    heads_per_slab = MXU_DIM // HEAD_DV
    slab_rows = heads_per_slab * rows_per_head
    s = jnp.concatenate(
        [_dot(qx_ref[g * slab_rows:(g + 1) * slab_rows, g * MXU_DIM:(g + 1) * MXU_DIM],
              kt[g * MXU_DIM:(g + 1) * MXU_DIM, :]) for g in range(d // MXU_DIM)], axis=0)
    alpha, pb = scores_update(s)
    facc_ref[...] += ffn_chunk(c)
    values_update(alpha, pb, cached_v)

    @pl.when(c + n_steps < n_cols)
    def _():
        facc_ref[...] += ffn_chunk(c + n_steps)

    @pl.when(c == n_steps - 1)
    def _():
        xo_ref[...] = x + gt_ref[...] * facc_ref[...].reshape(x.shape)
        pad = jnp.zeros((page - t_new, d), F32)
        kb_new = jnp.concatenate([kn_ref[...], pad], axis=0).astype(BF16)
        vb_new = jnp.concatenate([vn_ref[...], pad], axis=0).astype(BF16)
        s = _dot_nt(qx_ref[...], kb_new)
        row = lax.broadcasted_iota(jnp.int32, s.shape, 0)
        col = lax.broadcasted_iota(jnp.int32, s.shape, 1)
        s = jnp.where(col <= row % t_new, s, NEG_BIG)
        alpha_new, pb_new = scores_update(s)
        values_update(alpha_new, pb_new, lambda h: vb_new[:, h * HEAD_DV:(h + 1) * HEAD_DV])
        lam = _diff_lambda(lp_ref, lam_init)
        inv_l = 1.0 / l_ref[...]
        for h in range(N_HEADS):
            r0 = h * rows_per_head
            o0 = acc_ref[r0:r0 + t_new, :] * inv_l[r0:r0 + t_new]
            o1 = acc_ref[r0 + t_new:r0 + 2 * t_new, :] * inv_l[r0 + t_new:r0 + 2 * t_new]
            o_ref[:, h * HEAD_DV:(h + 1) * HEAD_DV] = o0 - lam * o1


def _attn_paged_ffn_layer(q_tm, kn_tm, vn_tm, cache_kt, cache_vr, page_table, lp, lam_init, n_pages,
                          x, tok, mods, l, norm_g, w13c, w2c):
    t_new, nb, d = q_tm.shape
    n_chunks = page_table.shape[1] // n_pages
    n_rows = 2 * N_HEADS * t_new
    assert tok.n_tiles == nb, "one FFN row tile rides with each sample sequence"
    assert w2c.shape[1] <= 2 * n_chunks, "at most two hidden-column chunks per step"
    ffn_rows = tok.tt * (tok.nb if tok.kind == "tm" else 1)
    row_spec = pl.BlockSpec((t_new, d), lambda b, c, pt: (0, b))

    def page_spec(p, shape):
        return pl.BlockSpec((None,) + shape, lambda b, c, pt: (pt[b, c * n_pages + p], 0, 0))

    k_pages = [page_spec(p, cache_kt.shape[1:]) for p in range(n_pages)]
    v_pages = [page_spec(p, cache_vr.shape[1:]) for p in range(n_pages)]
    (m_arr, sh), (_, sc), (_, gt) = tok.mods_flat(mods, l, 3)
    grid_spec = pltpu.PrefetchScalarGridSpec(
        num_scalar_prefetch=1,
        grid=(nb, n_chunks),
        in_specs=[row_spec, row_spec, row_spec,
                  pl.BlockSpec(lp.shape, lambda b, c, pt: (0, 0))] + k_pages + v_pages
        + [tok.spec_flat(), _const_spec((1, d)), sh, sc, gt, _layer_spec(w13c, l), _layer_spec(w2c, l)],
        out_specs=[row_spec, tok.spec_flat()],
        scratch_shapes=[pltpu.VMEM((n_rows, d), BF16), pltpu.VMEM((n_rows, LANES), F32),
                        pltpu.VMEM((n_rows, LANES), F32), pltpu.VMEM((n_rows, HEAD_DV), F32),
                        pltpu.VMEM((ffn_rows, d), BF16), pltpu.VMEM((ffn_rows, d), F32)],
    )
    flat = lambda a: a.reshape(t_new, nb * d)
    o, x_new = pl.pallas_call(
        functools.partial(_attn_paged_kernel, n_pages=n_pages, lam_init=lam_init),
        grid_spec=grid_spec,
        out_shape=[jax.ShapeDtypeStruct((t_new, nb * d), F32), tok.shape()],
        compiler_params=_params(2),
        name="diff_attn_paged_ffn",
    )(page_table, flat(q_tm), flat(kn_tm), flat(vn_tm), lp,
      *([cache_kt] * n_pages), *([cache_vr] * n_pages),
      x, norm_g, m_arr, m_arr, m_arr, w13c, w2c)
    return o.reshape(t_new, nb, d), x_new


def _attn_out_kernel(x_ref, o_ref_in, gt_ref, subw_ref, wo_ref, out_ref, *, lam_init):
    x = x_ref[...]
    d = x.shape[-1]
    o = o_ref_in[...].reshape(-1, d)
    heads = []
    for h in range(N_HEADS):
        hs = slice(h * HEAD_DV, (h + 1) * HEAD_DV)
        heads.append(_head_out(o[:, hs], subw_ref[:, hs], lam_init))
    ob = jnp.concatenate(heads, axis=1).astype(BF16)
    out_ref[...] = x + gt_ref[...] * _dot(ob, wo_ref[...]).reshape(x.shape)


def _attn_out_layer(x, o, tok, mods, l, j, subw, wo, lam_init):
    m_arr, gt = tok.mods(mods, l, 3)[2]
    return pl.pallas_call(
        functools.partial(_attn_out_kernel, lam_init=lam_init),
        grid=tok.grid,
        in_specs=[tok.spec(), tok.spec(), gt, _const_spec(subw.shape), _layer_spec(wo, j)],
        out_specs=tok.spec(),
        out_shape=tok.shape(),
        compiler_params=_params(len(tok.grid)),
        name="diff_attn_out",
    )(x, o, m_arr, subw, wo)


def kernel(x_prompt, x_sample, state_ssm_re, state_ssm_im, cache_k, cache_v, page_table, c_prompt, c_sample, norm_mix, w_ada_mix, b_ada_mix, norm_ffn, w_ada_ffn, b_ada_ffn, ffn_w13, ffn_w2, ssm_lam_re, ssm_lam_im, ssm_log_step, ssm_b_re, ssm_b_im, ssm_c_re, ssm_c_im, ssm_d, ssm_w_glu, norm_kv, w_ada_kv, b_ada_kv, w_kv, k_norm, w_q, q_norm, diff_lambda, subln, w_o):
    pb, seq, d = x_prompt.shape
    db, dseq, _ = x_sample.shape
    depth = norm_mix.shape[0]
    n_a, g, n = ssm_lam_re.shape
    gn = g * n
    nk = N_HEADS * 2 * HEAD_DK

    c_all = jnp.concatenate([c_sample, c_prompt], axis=0)
    mods_mix = _Mods(_ada_mods(c_all, w_ada_mix, b_ada_mix), d, db)
    mods_ffn = _Mods(_ada_mods(c_all, w_ada_ffn, b_ada_ffn), d, db)
    mods_kv = _Mods(_ada_mods(c_all, w_ada_kv[None], b_ada_kv[None]), d, db)

    a_re, a_im, wb = _discretize(ssm_lam_re, ssm_lam_im, ssm_log_step, ssm_b_re, ssm_b_im)
    wc = _block_diag_out(ssm_c_re, ssm_c_im)
    wglu, wkv, wq, wo = _to_bf16(ssm_w_glu), _to_bf16(w_kv), _to_bf16(w_q), _to_bf16(w_o)
    w13c = _to_bf16_col_chunks(ffn_w13, MXU_DIM)
    w2c = _to_bf16(ffn_w2).reshape(depth, -1, MXU_DIM, d)
    row = lambda a: a.reshape(1, -1)
    knw = row(jnp.tile(k_norm.reshape(-1), N_HEADS))
    qnw = [row(jnp.tile(q_norm[j].reshape(-1), N_HEADS)) for j in range(depth - n_a)]
    subw = [row(jnp.tile(subln[j], N_HEADS)) for j in range(depth - n_a)]
    blk = jnp.arange(MXU_DIM) // HEAD_DK
    ones_blk = ((blk[:, None] == blk[None, :]).astype(F32) / HEAD_DK).astype(BF16)
    lam_inits = [0.8 - 0.6 * math.exp(-0.3 * layer) for layer in range(n_a, depth)]

    n_pool, page = cache_k.shape[:2]
    ck = cache_k.transpose(0, 2, 3, 4, 1).reshape(n_pool, nk, page)
    cv = cache_v.reshape(n_pool, page * N_HEADS, HEAD_DV)

    n_b = depth - n_a
    assert n_b <= n_a, "every paged attention layer needs an S5-phase FFN of the prompt group to host"

    def s5(x, a, tc, bm_in, bm_out, row_block, h0_re, h0_im):
        return _s5_layer(x, tc, bm_in, bm_out, mods_mix, a, row_block, row(norm_mix[a]),
                         h0_re[a], h0_im[a], a_re[a], a_im[a], wb, wc, row(ssm_d[a]), wglu)

    def ffn(x, tok, layer):
        return _ffn_layer(x, tok, mods_ffn, layer, row(norm_ffn[layer]), w13c, w2c)

    def states(parts, nb):
        return jnp.stack([p.reshape(nb, g, n) for p in parts])

    tok_s = _Tokens("tm", db, dseq, d, db * dseq)
    xs = x_sample.transpose(1, 0, 2)
    hs_re, hs_im = state_ssm_re.reshape(n_a, db, gn).astype(F32), state_ssm_im.reshape(n_a, db, gn).astype(F32)
    re_s, im_s = [], []
    for a in range(n_a):
        xs, hr, hi = s5(xs, a, dseq, False, False, 0, hs_re, hs_im)
        re_s.append(hr)
        im_s.append(hi)
        xs = ffn(xs, tok_s, a)
    k_s, v_s, _, _ = _kv_layer(xs, tok_s, mods_kv, row(norm_kv), wkv, ones_blk, knw)

    tok_tm = _Tokens("tm", pb, seq, d, PROMPT_ROWS, mod_row_block=db // pb)
    tok_bm = _Tokens("bm", pb, seq, d, PROMPT_ROWS)
    zeros = jnp.zeros((n_a, pb, gn), F32)
    xp = x_prompt
    re_p, im_p = [], []
    for a in range(n_a):
        last = a == n_a - 1
        xp, hr, hi = s5(xp, a, PROMPT_ROWS // pb, a == 0, last, db // pb, zeros, zeros)
        re_p.append(hr)
        im_p.append(hi)
        tok = tok_bm if last else tok_tm
        if a < n_b:
            j, layer = a, n_a + a
            q = _query_layer(xs, tok_s, mods_mix, layer, j, row(norm_mix[layer]), wq, ones_blk, qnw[j])
            o, xp = _attn_paged_ffn_layer(q, k_s, v_s, ck, cv, page_table, diff_lambda[j], lam_inits[j],
                                          PAGES_PER_STEP, xp, tok, mods_ffn, a, row(norm_ffn[a]),
                                          w13c, w2c)
            xs = _attn_out_layer(xs, o, tok_s, mods_mix, layer, j, subw[j], wo, lam_inits[j])
            xs = ffn(xs, tok_s, layer)
        else:
            xp = ffn(xp, tok, a)

    k_p, v_p, kb, vb = _kv_layer(xp, tok_bm, mods_kv, row(norm_kv), wkv, ones_blk, knw)
    for j in range(n_b):
        layer = n_a + j
        xp = _attn_prompt_layer(xp, tok_bm, mods_mix, layer, j, row(norm_mix[layer]), wq, ones_blk,
                                qnw[j], kb, vb, diff_lambda[j], subw[j], wo, lam_inits[j])
        xp = ffn(xp, tok_bm, layer)

    y_p, y_s = xp, xs
    re_p, im_p, re_s, im_s = states(re_p, pb), states(im_p, pb), states(re_s, db), states(im_s, db)
    tm2bm = lambda a: a.transpose(1, 0, 2)

    return (y_p, tm2bm(y_s), re_p, im_p, re_s, im_s,
            k_p.reshape(pb, seq, N_HEADS, 2, HEAD_DK), v_p.reshape(pb, seq, N_HEADS, HEAD_DV),
            tm2bm(k_s).reshape(db, dseq, N_HEADS, 2, HEAD_DK), tm2bm(v_s).reshape(db, dseq, N_HEADS, HEAD_DV))
```
